```python
import jax, jax.numpy as jnp
from jax import lax
import numpy as np

D_MODEL = 1024
BATCH = 8
SEQ = 2048
DEPTH = 4

GRID_W = 64
CTX_LEN = 256
BLOCK = 128
WINDOW = 128
ROPE_THETA = 10000.0
EPS = 1e-6
NEG = -1e30

A_HEADS = 8
A_KV = 2
A_HD = 128
B_HEADS = 8
B_KV = 2
B_HD = 128
C_HEADS = 4
C_DK = 256
C_DV = 512
D_FF = 4 * D_MODEL
N_MOD = 6

A_Q = A_HEADS * A_HD
A_KVW = A_KV * A_HD
B_Q = B_HEADS * B_HD
B_KVW = B_KV * B_HD
C_QK = C_HEADS * C_DK
C_V = C_HEADS * C_DV
IN_SIZES = (A_Q, A_KVW, A_KVW, B_Q, B_KVW, B_KVW, C_QK, C_QK, C_V, C_V, D_MODEL, D_MODEL, D_MODEL)
IN_WIDTH = sum(IN_SIZES)

kernel_name = "hybrid_parallel_gqa_window_retention_dit"


def rmsnorm(x, g):
    xf = x.astype(jnp.float32)
    y = xf * lax.rsqrt(jnp.mean(xf * xf, axis=-1, keepdims=True) + EPS)
    return (y * g).astype(x.dtype)


def split_cols(u):
    offs = np.cumsum(IN_SIZES)[:-1].tolist()
    return jnp.split(u, offs, axis=-1)


def heads(x, n, d):
    return x.reshape(x.shape[:-1] + (n, d))


def group(q, n_kv):
    return q.reshape(q.shape[:2] + (n_kv, q.shape[2] // n_kv, q.shape[3]))


def axial_rope(L, dim):
    rows = L // GRID_W
    row = jnp.repeat(jnp.arange(rows), GRID_W).astype(jnp.float32)
    col = jnp.tile(jnp.arange(GRID_W), rows).astype(jnp.float32)
    n = dim // 4
    inv = ROPE_THETA ** (-jnp.arange(n, dtype=jnp.float32) / n)
    ang = jnp.concatenate([row[:, None] * inv, col[:, None] * inv], axis=-1)
    return jnp.cos(ang)[:, None, :], jnp.sin(ang)[:, None, :]


def rope(x, cs):
    cos, sin = cs
    half = x.shape[-1] // 2
    x1, x2 = x[..., :half], x[..., half:]
    out = jnp.concatenate([x1 * cos - x2 * sin, x1 * sin + x2 * cos], axis=-1)
    return out.astype(x.dtype)


def attend(q, k, v, sink=None, mask=None):
    s = jnp.einsum('bqkgd,bskd->bkgqs', q, k).astype(jnp.float32) * (q.shape[-1] ** -0.5)
    if mask is not None:
        s = jnp.where(mask, s, NEG)
    if sink is not None:
        col = jnp.broadcast_to(sink.astype(jnp.float32)[None, :, :, None, None], s.shape[:-1] + (1,))
        p = jax.nn.softmax(jnp.concatenate([s, col], axis=-1), axis=-1)[..., :-1]
    else:
        p = jax.nn.softmax(s, axis=-1)
    return jnp.einsum('bkgqs,bskd->bqkgd', p.astype(v.dtype), v)


def flat(o):
    return o.reshape(o.shape[:2] + (-1,))


def merge_blocks(o):
    o = jnp.moveaxis(o, 0, 1)
    return o.reshape((o.shape[0], o.shape[1] * o.shape[2], -1))


def mixer_a(q, k, v, qc, kc, vc, qn_g, kn_g, cs, need_ctx):
    L = q.shape[1]
    q = group(rope(rmsnorm(heads(q, A_HEADS, A_HD), qn_g), cs), A_KV)
    k = rope(rmsnorm(heads(k, A_KV, A_HD), kn_g), cs)
    v = heads(v, A_KV, A_HD)
    qc = group(rmsnorm(heads(qc, A_HEADS, A_HD), qn_g), A_KV)
    kc = rmsnorm(heads(kc, A_KV, A_HD), kn_g)
    vc = heads(vc, A_KV, A_HD)
    k_all = jnp.concatenate([k, kc], axis=1)
    v_all = jnp.concatenate([v, vc], axis=1)

    def block(i):
        qi = lax.dynamic_slice_in_dim(q, i * BLOCK, BLOCK, axis=1)
        return attend(qi, k_all, v_all)

    y = merge_blocks(lax.map(block, jnp.arange(L // BLOCK)))
    yc = flat(attend(qc, kc, vc)) if need_ctx else None
    return y, yc


def mixer_b(q, k, v, qc, kc, vc, sink, cs, need_ctx):
    L = q.shape[1]
    q = group(rope(heads(q, B_HEADS, B_HD), cs), B_KV)
    k = rope(heads(k, B_KV, B_HD), cs)
    v = heads(v, B_KV, B_HD)
    qc = group(heads(qc, B_HEADS, B_HD), B_KV)
    kc = heads(kc, B_KV, B_HD)
    vc = heads(vc, B_KV, B_HD)
    snk = sink.reshape(B_KV, B_HEADS // B_KV)
    pad = [(0, 0), (BLOCK, BLOCK), (0, 0), (0, 0)]
    kp = jnp.pad(k, pad)
    vp = jnp.pad(v, pad)
    a = jnp.arange(BLOCK)
    j = jnp.arange(3 * BLOCK)
    band = jnp.abs(a[:, None] + BLOCK - j[None, :]) <= WINDOW
    ctx_ok = jnp.ones((BLOCK, kc.shape[1]), dtype=bool)

    def block(i):
        qi = lax.dynamic_slice_in_dim(q, i * BLOCK, BLOCK, axis=1)
        kw = jnp.concatenate([lax.dynamic_slice_in_dim(kp, i * BLOCK, 3 * BLOCK, axis=1), kc], axis=1)
        vw = jnp.concatenate([lax.dynamic_slice_in_dim(vp, i * BLOCK, 3 * BLOCK, axis=1), vc], axis=1)
        pos = i * BLOCK + j - BLOCK
        valid = band & ((pos >= 0) & (pos < L))[None, :]
        m = jnp.concatenate([valid, ctx_ok], axis=1)
        return attend(qi, kw, vw, snk, m)

    y = merge_blocks(lax.map(block, jnp.arange(L // BLOCK)))
    yc = flat(attend(qc, kc, vc, snk)) if need_ctx else None
    return y, yc


def retention(q, k, v, log_gamma, state):
    bsz, L, H, _ = q.shape
    dv = v.shape[-1]
    n = L // BLOCK
    q, k, v = [t.astype(jnp.float32).reshape(bsz, n, BLOCK, H, -1).swapaxes(0, 1) for t in (q, k, v)]
    idx = jnp.arange(BLOCK, dtype=jnp.float32)
    diff = idx[:, None] - idx[None, :]
    decay_in = jnp.where(diff >= 0, jnp.exp(jnp.maximum(diff, 0.0)[None] * log_gamma[:, None, None]), 0.0)
    q_dec = jnp.exp((idx[:, None] + 1.0) * log_gamma[None, :])
    k_dec = jnp.exp((BLOCK - 1.0 - idx)[:, None] * log_gamma[None, :])
    c_dec = jnp.exp(BLOCK * log_gamma)

    def step(S, xs):
        qb, kb, vb = xs
        att = jnp.einsum('bqhd,bshd->bhqs', qb, kb) * decay_in
        o = jnp.einsum('bhqs,bshe->bqhe', att, vb) + jnp.einsum('bqhd,bhde->bqhe', qb * q_dec[:, :, None], S)
        S = S * c_dec[:, None, None] + jnp.einsum('bshd,bshe->bhde', kb * k_dec[:, :, None], vb)
        return S, o

    S, o = lax.scan(step, state, (q, k, v))
    return o.swapaxes(0, 1).reshape(bsz, L, H, dv), S


def mixer_c(q, k, v, g, qc, kc, vc, gc, logit, norm_g, cs, need_ctx):
    scale = C_DK ** -0.5
    q = rope(heads(q, C_HEADS, C_DK), cs)
    k = rope(heads(k, C_HEADS, C_DK), cs) * scale
    v = heads(v, C_HEADS, C_DV)
    qc = heads(qc, C_HEADS, C_DK)
    kc = heads(kc, C_HEADS, C_DK) * scale
    vc = heads(vc, C_HEADS, C_DV)
    lg = jax.nn.log_sigmoid(logit.astype(jnp.float32))
    zero = jnp.zeros((q.shape[0], C_HEADS, C_DK, C_DV), jnp.float32)
    fl = lambda t: jnp.flip(t, axis=1)
    oc_f, s_f = retention(qc, kc, vc, lg[0], zero)
    oc_b, s_b = retention(fl(qc), fl(kc), fl(vc), lg[1], zero)
    o_f, _ = retention(q, k, v, lg[0], s_f)
    o_b, _ = retention(fl(q), fl(k), fl(v), lg[1], s_b)
    gn = norm_g.reshape(C_HEADS, C_DV)

    def out(o, gate):
        o = rmsnorm(o.astype(gate.dtype), gn)
        return o.reshape(o.shape[:2] + (C_V,)) * jax.nn.silu(gate)

    y = out(o_f + fl(o_b), g)
    yc = out(oc_f + fl(oc_b), gc) if need_ctx else None
    return y, yc


def merge(ya, yb, yc, gates, w_pa, w_pb, w_pc, w_o):
    ga, gb, gc = gates
    y = jax.nn.sigmoid(ga) * (ya @ w_pa) + jax.nn.sigmoid(gb) * (yb @ w_pb) + jax.nn.sigmoid(gc) * (yc @ w_pc)
    return y @ w_o


def mlp(h, w1, w2):
    return jnp.square(jax.nn.relu(h @ w1)) @ w2


def setup_inputs(seed: int = 0) -> dict:
    key = jax.random.key(seed)
    ks = jax.random.split(key, 24)
    f32 = jnp.float32
    nrm = lambda k, shape, fan: jax.random.normal(k, shape, f32) * (fan ** -0.5)
    gain = lambda k, shape: 1.0 + 0.05 * jax.random.normal(k, shape, f32)
    gamma0 = 1.0 - 2.0 ** (-5.0 - jnp.arange(C_HEADS, dtype=f32))
    base_logit = jnp.log(gamma0) - jnp.log1p(-gamma0)
    return {
        "x": jax.random.normal(ks[0], (BATCH, SEQ, D_MODEL), f32),
        "c": jax.random.normal(ks[1], (BATCH, D_MODEL), f32),
        "ctx": jax.random.normal(ks[2], (BATCH, CTX_LEN, D_MODEL), f32),
        "c_ctx": jax.random.normal(ks[3], (D_MODEL,), f32),
        "w_ada": 0.5 * nrm(ks[4], (DEPTH, D_MODEL, N_MOD * D_MODEL), D_MODEL),
        "b_ada": 0.02 * jax.random.normal(ks[5], (DEPTH, N_MOD * D_MODEL), f32),
        "norm1_g": gain(ks[6], (DEPTH, D_MODEL)),
        "norm2_g": gain(ks[7], (DEPTH, D_MODEL)),
        "w_in": nrm(ks[8], (DEPTH, D_MODEL, IN_WIDTH), D_MODEL),
        "qn_g": gain(ks[9], (DEPTH, A_HD)),
        "kn_g": gain(ks[10], (DEPTH, A_HD)),
        "sink": jax.random.normal(ks[11], (DEPTH, B_HEADS), f32),
        "ret_logit": base_logit + 0.1 * jax.random.normal(ks[12], (DEPTH, 2, C_HEADS), f32),
        "ret_norm_g": gain(ks[13], (DEPTH, C_V)),
        "w_pa": nrm(ks[14], (DEPTH, A_Q, D_MODEL), A_Q),
        "w_pb": nrm(ks[15], (DEPTH, B_Q, D_MODEL), B_Q),
        "w_pc": nrm(ks[16], (DEPTH, C_V, D_MODEL), C_V),
        "w_o": nrm(ks[17], (DEPTH, D_MODEL, D_MODEL), D_MODEL),
        "w_ff1": nrm(ks[18], (DEPTH, D_MODEL, D_FF), D_MODEL),
        "w_ff2": nrm(ks[19], (DEPTH, D_FF, D_MODEL), D_FF),
        "final_g": gain(ks[20], (D_MODEL,)),
    }


def reference(x, c, ctx, c_ctx, w_ada, b_ada, norm1_g, norm2_g, w_in, qn_g, kn_g, sink,
              ret_logit, ret_norm_g, w_pa, w_pb, w_pc, w_o, w_ff1, w_ff2, final_g):
    L = x.shape[1]
    cs_h = axial_rope(L, A_HD)
    cs_r = axial_rope(L, C_DK)
    s_lat = jax.nn.silu(c)
    s_ctx = jax.nn.silu(c_ctx)[None]
    xc = ctx
    for l in range(DEPTH):
        need_ctx = l < DEPTH - 1
        mod = (s_lat @ w_ada[l] + b_ada[l])[:, None, :]
        modc = (s_ctx @ w_ada[l] + b_ada[l])[:, None, :]
        sh1, sc1, g1, sh2, sc2, g2 = jnp.split(mod, N_MOD, axis=-1)
        sh1c, sc1c, g1c, sh2c, sc2c, g2c = jnp.split(modc, N_MOD, axis=-1)

        h = rmsnorm(x, norm1_g[l]) * (1.0 + sc1) + sh1
        hc = rmsnorm(xc, norm1_g[l]) * (1.0 + sc1c) + sh1c
        u = split_cols(h @ w_in[l])
        uc = split_cols(hc @ w_in[l])

        ya, yac = mixer_a(u[0], u[1], u[2], uc[0], uc[1], uc[2], qn_g[l], kn_g[l], cs_h, need_ctx)
        yb, ybc = mixer_b(u[3], u[4], u[5], uc[3], uc[4], uc[5], sink[l], cs_h, need_ctx)
        yr, yrc = mixer_c(u[6], u[7], u[8], u[9], uc[6], uc[7], uc[8], uc[9],
                          ret_logit[l], ret_norm_g[l], cs_r, need_ctx)

        x = x + g1 * merge(ya, yb, yr, u[10:13], w_pa[l], w_pb[l], w_pc[l], w_o[l])
        x = x + g2 * mlp(rmsnorm(x, norm2_g[l]) * (1.0 + sc2) + sh2, w_ff1[l], w_ff2[l])
        if need_ctx:
            xc = xc + g1c * merge(yac, ybc, yrc, uc[10:13], w_pa[l], w_pb[l], w_pc[l], w_o[l])
            xc = xc + g2c * mlp(rmsnorm(xc, norm2_g[l]) * (1.0 + sc2c) + sh2c, w_ff1[l], w_ff2[l])
    return rmsnorm(x, final_g)
```

```python
import functools

import jax
import jax.numpy as jnp
from jax import lax
from jax.experimental import pallas as pl
from jax.experimental.pallas import tpu as pltpu

F32 = jnp.float32
BF16 = jnp.bfloat16

GRID_W = 64
WINDOW = 128
ROPE_THETA = 10000.0
EPS = 1e-6
NEG = -1e30
A_HEADS, A_KV, A_HD = 8, 2, 128
B_HEADS, B_KV, B_HD = 8, 2, 128
C_HEADS, C_DK, C_DV = 4, 256, 512
N_MOD = 6
CHUNK = 128

LANES = 128
V7X_VMEM_BYTES = 64 * 1024 * 1024
VMEM_LIMIT = V7X_VMEM_BYTES - 8 * 1024 * 1024

ROW_BLOCK = 768
TQ = 256
UNIT = LANES

U_AQ, U_BQ = 0, 8
U_AK, U_BK, U_AV, U_BV = 16, 18, 20, 22
U_CQ, U_CK, U_CV, U_CG = 24, 32, 40, 56
U_GA, U_GB, U_GC = 72, 80, 88
N_UNITS = 96
_UNIT_PERM = (list(range(0, 8)) + list(range(12, 20)) + [8, 9, 20, 21, 10, 11, 22, 23]
              + list(range(24, 96)))
IN_BLOCK_UNITS = 8


def _params(*sem):
    return pltpu.CompilerParams(dimension_semantics=sem, vmem_limit_bytes=VMEM_LIMIT)


def _rms(x):
    return x * lax.rsqrt(jnp.mean(x * x, axis=-1, keepdims=True) + EPS)


def _sigmoid(x):
    return 1.0 / (1.0 + jnp.exp(-x))


def _ctx_rows(j, rows, n_lat):
    r = j * rows + lax.broadcasted_iota(jnp.int32, (rows, 1), 0)
    return r >= n_lat


def _mod(modb_ref, modc_ref, idx, is_ctx):
    return jnp.where(is_ctx, modc_ref[idx:idx + 1, :], modb_ref[idx:idx + 1, :])


def _ada_kernel(c_ref, w_ref, b_ref, o_ref):
    c = c_ref[...]
    s = (c * _sigmoid(c)).astype(BF16)
    o_ref[...] = jnp.dot(s, w_ref[...].astype(BF16), preferred_element_type=F32) + b_ref[...]


def _ada(craw, w_ada, b_ada):
    depth, d, width = w_ada.shape
    rows = craw.shape[0]
    tn = 1536
    return pl.pallas_call(
        _ada_kernel,
        grid=(depth, width // tn),
        in_specs=[
            pl.BlockSpec((rows, d), lambda l, j: (0, 0)),
            pl.BlockSpec((None, d, tn), lambda l, j: (l, 0, j)),
            pl.BlockSpec((None, 1, tn), lambda l, j: (l, 0, j)),
        ],
        out_specs=pl.BlockSpec((None, rows, tn), lambda l, j: (l, 0, j)),
        out_shape=jax.ShapeDtypeStruct((depth, rows, width), F32),
        compiler_params=_params("parallel", "parallel"),
        name="ada",
    )(craw, w_ada, b_ada.reshape(depth, 1, width))


def _prenorm_kernel(x_ref, modb_ref, modc_ref, g_ref, h_ref, *, n_lat):
    rows = x_ref.shape[0]
    is_ctx = _ctx_rows(pl.program_id(1), rows, n_lat)
    y = _rms(x_ref[...]) * g_ref[...]
    h = y * (1.0 + _mod(modb_ref, modc_ref, 1, is_ctx)) + _mod(modb_ref, modc_ref, 0, is_ctx)
    h_ref[...] = h.astype(h_ref.dtype)


def _mod_specs(d, n_batch):
    return [
        pl.BlockSpec((None, N_MOD, d), lambda b, j: (b, 0, 0)),
        pl.BlockSpec((None, N_MOD, d), lambda b, j: (n_batch, 0, 0)),
    ]


def _prenorm(x, mod, g, n_lat):
    nb, t, d = x.shape
    row = lambda b, j: (b, j, 0)
    return pl.pallas_call(
        functools.partial(_prenorm_kernel, n_lat=n_lat),
        grid=(nb, t // ROW_BLOCK),
        in_specs=[pl.BlockSpec((None, ROW_BLOCK, d), row)] + _mod_specs(d, nb)
        + [pl.BlockSpec((1, d), lambda b, j: (0, 0))],
        out_specs=pl.BlockSpec((None, ROW_BLOCK, d), row),
        out_shape=jax.ShapeDtypeStruct((nb, t, d), BF16),
        compiler_params=_params("parallel", "parallel"),
        name="prenorm",
    )(x, mod, mod, g.reshape(1, d))


def _rope128(a, c, s):
    return a * c + pltpu.roll(a, LANES // 2, axis=1) * s


def _inproj_kernel(h_ref, w_ref, c128_ref, s128_ref, c256_ref, s256_ref, qn_ref, kn_ref, o_ref):
    j = pl.program_id(1)
    t = h_ref.shape[0]
    nchunk = t // ROW_BLOCK

    def for_chunks(fn):
        for ci in range(nchunk):
            rows = pl.ds(ci * ROW_BLOCK, ROW_BLOCK)
            acc = jnp.dot(h_ref[rows, :], w_ref[...], preferred_element_type=F32)
            for u in range(IN_BLOCK_UNITS):
                cols = slice(u * UNIT, (u + 1) * UNIT)
                out = fn(u, acc, rows)
                if out is not None:
                    o_ref[rows, cols] = out.astype(o_ref.dtype)

    def unit(acc, u):
        return acc[:, u * UNIT:(u + 1) * UNIT]

    def rope_a(a, rows):
        return _rope128(a, c128_ref[rows, :], s128_ref[rows, :])

    @pl.when(j == U_AQ // IN_BLOCK_UNITS)
    def _():
        scale = A_HD ** -0.5
        for_chunks(lambda u, acc, rows: rope_a(_rms(unit(acc, u)) * qn_ref[...], rows) * scale)

    @pl.when(j == U_BQ // IN_BLOCK_UNITS)
    def _():
        scale = B_HD ** -0.5
        for_chunks(lambda u, acc, rows: rope_a(unit(acc, u), rows) * scale)

    @pl.when(j == U_AK // IN_BLOCK_UNITS)
    def _():
        def fn(u, acc, rows):
            a = unit(acc, u)
            if u < U_BK - U_AK:
                return rope_a(_rms(a) * kn_ref[...], rows)
            if u < U_AV - U_AK:
                return rope_a(a, rows)
            return a
        for_chunks(fn)

    def rope_c(u, acc, rows, scale):
        if u % 2:
            return None
        a1, a2 = unit(acc, u), unit(acc, u + 1)
        c, s = c256_ref[rows, :], s256_ref[rows, :]
        o_ref[rows, (u + 1) * UNIT:(u + 2) * UNIT] = ((a1 * s + a2 * c) * scale).astype(o_ref.dtype)
        return (a1 * c - a2 * s) * scale

    @pl.when(j == U_CQ // IN_BLOCK_UNITS)
    def _():
        for_chunks(lambda u, acc, rows: rope_c(u, acc, rows, 1.0))

    @pl.when(j == U_CK // IN_BLOCK_UNITS)
    def _():
        for_chunks(lambda u, acc, rows: rope_c(u, acc, rows, C_DK ** -0.5))

    @pl.when(j >= U_CV // IN_BLOCK_UNITS)
    def _():
        for_chunks(lambda u, acc, rows: unit(acc, u))


def _inproj(h, w, tabs, qn, kn):
    nb, t, d = h.shape
    width = w.shape[1]
    tn = IN_BLOCK_UNITS * UNIT
    tab = pl.BlockSpec((t, LANES), lambda b, j: (0, 0))
    gain = pl.BlockSpec((1, LANES), lambda b, j: (0, 0))
    return pl.pallas_call(
        _inproj_kernel,
        grid=(nb, width // tn),
        in_specs=[
            pl.BlockSpec((None, t, d), lambda b, j: (b, 0, 0)),
            pl.BlockSpec((d, tn), lambda b, j: (0, j)),
            tab, tab, tab, tab, gain, gain,
        ],
        out_specs=pl.BlockSpec((None, t, tn), lambda b, j: (b, 0, j)),
        out_shape=jax.ShapeDtypeStruct((nb, t, width), BF16),
        compiler_params=_params("parallel", "arbitrary"),
        name="inproj",
    )(h, w, *tabs, qn.reshape(1, LANES), kn.reshape(1, LANES))


def _qk(q, k):
    return lax.dot_general(q, k, (((1,), (1,)), ((), ())), preferred_element_type=F32)


def _attn_a_kernel(q_ref, k_ref, v_ref, o_ref, *, n_lat, group, hd):
    i = pl.program_id(2)

    def run(k, v):
        for g in range(group):
            cols = slice(g * hd, (g + 1) * hd)
            s = _qk(q_ref[:, cols], k)
            p = jnp.exp(s - jnp.max(s, axis=-1, keepdims=True))
            inv = 1.0 / jnp.sum(p, axis=-1, keepdims=True)
            o = jnp.dot(p.astype(BF16), v, preferred_element_type=F32)
            o_ref[:, cols] = (o * inv).astype(o_ref.dtype)

    @pl.when(i < n_lat // TQ)
    def _():
        run(k_ref[...], v_ref[...])

    @pl.when(i >= n_lat // TQ)
    def _():
        run(k_ref[n_lat:, :], v_ref[n_lat:, :])


def _attn_a(u, n_lat):
    nb, t, _ = u.shape
    group = A_HEADS // A_KV
    qw = group * A_HD
    return pl.pallas_call(
        functools.partial(_attn_a_kernel, n_lat=n_lat, group=group, hd=A_HD),
        grid=(nb, A_KV, t // TQ),
        in_specs=[
            pl.BlockSpec((None, TQ, qw), lambda b, kv, i: (b, i, U_AQ * UNIT // qw + kv)),
            pl.BlockSpec((None, t, A_HD), lambda b, kv, i: (b, 0, U_AK + kv)),
            pl.BlockSpec((None, t, A_HD), lambda b, kv, i: (b, 0, U_AV + kv)),
        ],
        out_specs=pl.BlockSpec((None, TQ, qw), lambda b, kv, i: (b, i, kv)),
        out_shape=jax.ShapeDtypeStruct((nb, t, A_HEADS * A_HD), BF16),
        compiler_params=_params("parallel", "parallel", "arbitrary"),
        name="attn_a",
    )(u, u, u)


def _attn_b_kernel(sink_ref, q_ref, k_ref, v_ref, o_ref, *, n_lat, group, hd):
    kv = pl.program_id(1)
    i = pl.program_id(2)
    span = TQ + 2 * WINDOW
    kc = k_ref[n_lat:, :]
    vc = v_ref[n_lat:, :]

    def finish(g, s_c, s_w, vw):
        cols = slice(g * hd, (g + 1) * hd)
        snk = sink_ref[kv * group + g]
        m = jnp.maximum(jnp.max(s_c, axis=-1, keepdims=True), snk)
        if s_w is not None:
            m = jnp.maximum(m, jnp.max(s_w, axis=-1, keepdims=True))
        p_c = jnp.exp(s_c - m)
        den = jnp.sum(p_c, axis=-1, keepdims=True) + jnp.exp(snk - m)
        o = jnp.dot(p_c.astype(BF16), vc, preferred_element_type=F32)
        if s_w is not None:
            p_w = jnp.exp(s_w - m)
            den = den + jnp.sum(p_w, axis=-1, keepdims=True)
            o = o + jnp.dot(p_w.astype(BF16), vw, preferred_element_type=F32)
        o_ref[:, cols] = (o * (1.0 / den)).astype(o_ref.dtype)

    @pl.when(i < n_lat // TQ)
    def _():
        start = pl.multiple_of(jnp.clip(i * TQ - WINDOW, 0, n_lat - span), WINDOW)
        kw = k_ref[pl.ds(start, span), :]
        vw = v_ref[pl.ds(start, span), :]
        qpos = i * TQ + lax.broadcasted_iota(jnp.int32, (TQ, span), 0)
        kpos = start + lax.broadcasted_iota(jnp.int32, (TQ, span), 1)
        valid = jnp.abs(qpos - kpos) <= WINDOW
        for g in range(group):
            q = q_ref[:, g * hd:(g + 1) * hd]
            finish(g, _qk(q, kc), jnp.where(valid, _qk(q, kw), NEG), vw)

    @pl.when(i >= n_lat // TQ)
    def _():
        for g in range(group):
            finish(g, _qk(q_ref[:, g * hd:(g + 1) * hd], kc), None, None)


def _attn_b(sink, u, n_lat):
    nb, t, _ = u.shape
    group = B_HEADS // B_KV
    qw = group * B_HD
    return pl.pallas_call(
        functools.partial(_attn_b_kernel, n_lat=n_lat, group=group, hd=B_HD),
        grid=(nb, B_KV, t // TQ),
        in_specs=[
            pl.BlockSpec(memory_space=pltpu.SMEM),
            pl.BlockSpec((None, TQ, qw), lambda b, kv, i: (b, i, U_BQ * UNIT // qw + kv)),
            pl.BlockSpec((None, t, B_HD), lambda b, kv, i: (b, 0, U_BK + kv)),
            pl.BlockSpec((None, t, B_HD), lambda b, kv, i: (b, 0, U_BV + kv)),
        ],
        out_specs=pl.BlockSpec((None, TQ, qw), lambda b, kv, i: (b, i, kv)),
        out_shape=jax.ShapeDtypeStruct((nb, t, B_HEADS * B_HD), BF16),
        compiler_params=_params("parallel", "parallel", "arbitrary"),
        name="attn_b",
    )(sink, u, u, u)


def _retention_kernel(logit_ref, q_ref, k_ref, v_ref, g_ref, gn_ref, o_ref, of_scr, s_scr, *, n_lat):
    h = pl.program_id(1)
    t = q_ref.shape[0]
    n_chunks = t // CHUNK
    n_lat_chunks = n_lat // CHUNK

    def log_gamma(direction):
        x = jnp.full((1, LANES), logit_ref[direction, h], F32)
        return jnp.minimum(x, 0.0) - jnp.log1p(jnp.exp(-jnp.abs(x)))

    qi = lax.broadcasted_iota(jnp.int32, (CHUNK, CHUNK), 0).astype(F32)
    si = lax.broadcasted_iota(jnp.int32, (CHUNK, CHUNK), 1).astype(F32)
    ri = lax.broadcasted_iota(jnp.int32, (CHUNK, 1), 0).astype(F32)

    def step(c, lg, lag, q_pow, k_pow):
        rows = pl.ds(pl.multiple_of(c * CHUNK, CHUNK), CHUNK)
        lg1 = lg[:, :1]
        decay = jnp.where(lag >= 0, jnp.exp(jnp.maximum(lag, 0.0) * lg), 0.0)
        qb, kb, vb = q_ref[rows, :], k_ref[rows, :], v_ref[rows, :]
        state = s_scr[...]
        att = _qk(qb, kb) * decay
        o = jnp.dot(att.astype(BF16), vb, preferred_element_type=F32)
        o = o + jnp.dot(qb, state.astype(BF16), preferred_element_type=F32) * jnp.exp(q_pow * lg1)
        kd = (kb.astype(F32) * jnp.exp(k_pow * lg1)).astype(BF16)
        s_scr[...] = state * jnp.exp(CHUNK * lg1) + lax.dot_general(
            kd, vb, (((0,), (0,)), ((), ())), preferred_element_type=F32)
        return rows, o

    s_scr[...] = jnp.zeros_like(s_scr)
    lg_f = log_gamma(0)

    def fwd(i, carry):
        c = lax.rem(i + n_lat_chunks, n_chunks)
        rows, o = step(c, lg_f, qi - si, ri + 1.0, CHUNK - 1.0 - ri)
        of_scr[rows, :] = o
        return carry

    lax.fori_loop(0, n_chunks, fwd, 0)

    s_scr[...] = jnp.zeros_like(s_scr)
    lg_b = log_gamma(1)

    def bwd(i, carry):
        c = n_chunks - 1 - i
        rows, o = step(c, lg_b, si - qi, CHUNK - ri, ri)
        o = o + of_scr[rows, :]
        gate = g_ref[rows, :].astype(F32)
        y = _rms(o) * gn_ref[...] * (gate * _sigmoid(gate))
        o_ref[rows, :] = y.astype(o_ref.dtype)
        return carry

    lax.fori_loop(0, n_chunks, bwd, 0)


def _retention(logit, u, gn, n_lat):
    nb, t, _ = u.shape
    return pl.pallas_call(
        functools.partial(_retention_kernel, n_lat=n_lat),
        grid=(nb, C_HEADS),
        in_specs=[
            pl.BlockSpec(memory_space=pltpu.SMEM),
            pl.BlockSpec((None, t, C_DK), lambda b, h: (b, 0, U_CQ * UNIT // C_DK + h)),
            pl.BlockSpec((None, t, C_DK), lambda b, h: (b, 0, U_CK * UNIT // C_DK + h)),
            pl.BlockSpec((None, t, C_DV), lambda b, h: (b, 0, U_CV * UNIT // C_DV + h)),
            pl.BlockSpec((None, t, C_DV), lambda b, h: (b, 0, U_CG * UNIT // C_DV + h)),
            pl.BlockSpec((1, C_DV), lambda b, h: (0, h)),
        ],
        out_specs=pl.BlockSpec((None, t, C_DV), lambda b, h: (b, 0, h)),
        out_shape=jax.ShapeDtypeStruct((nb, t, C_HEADS * C_DV), BF16),
        scratch_shapes=[pltpu.VMEM((t, C_DV), F32), pltpu.VMEM((C_DK, C_DV), F32)],
        compiler_params=_params("parallel", "parallel"),
        name="retention",
    )(logit, u, u, u, u, gn.reshape(1, C_HEADS * C_DV))


def _merge_kernel(ya_ref, yb_ref, yc_ref, ga_ref, gb_ref, gc_ref, x_ref, modb_ref, modc_ref, g2_ref,
                  wpa_ref, wpb_ref, wpc_ref, wo_ref, x_out_ref, h_out_ref, *, n_lat):
    rows = x_ref.shape[0]
    is_ctx = _ctx_rows(pl.program_id(1), rows, n_lat)

    def branch(y_ref, w_ref, gate_ref):
        p = jnp.dot(y_ref[...], w_ref[...], preferred_element_type=F32)
        return _sigmoid(gate_ref[...].astype(F32)) * p

    y = branch(ya_ref, wpa_ref, ga_ref) + branch(yb_ref, wpb_ref, gb_ref) + branch(yc_ref, wpc_ref, gc_ref)
    o = jnp.dot(y.astype(BF16), wo_ref[...], preferred_element_type=F32)
    x = x_ref[...] + _mod(modb_ref, modc_ref, 2, is_ctx) * o
    x_out_ref[...] = x
    h = _rms(x) * g2_ref[...]
    h = h * (1.0 + _mod(modb_ref, modc_ref, 4, is_ctx)) + _mod(modb_ref, modc_ref, 3, is_ctx)
    h_out_ref[...] = h.astype(h_out_ref.dtype)


def _merge(ya, yb, yc, u, x, mod, g2, wpa, wpb, wpc, wo, n_lat):
    nb, t, d = x.shape
    row = lambda b, j: (b, j, 0)
    full = lambda b, j: (0, 0)
    gate = lambda unit: pl.BlockSpec((None, ROW_BLOCK, d), lambda b, j: (b, j, unit * UNIT // d))
    return pl.pallas_call(
        functools.partial(_merge_kernel, n_lat=n_lat),
        grid=(nb, t // ROW_BLOCK),
        in_specs=[
            pl.BlockSpec((None, ROW_BLOCK, ya.shape[2]), row),
            pl.BlockSpec((None, ROW_BLOCK, yb.shape[2]), row),
            pl.BlockSpec((None, ROW_BLOCK, yc.shape[2]), row),
            gate(U_GA), gate(U_GB), gate(U_GC),
            pl.BlockSpec((None, ROW_BLOCK, d), row),
        ] + _mod_specs(d, nb) + [
            pl.BlockSpec((1, d), full),
            pl.BlockSpec(wpa.shape, full),
            pl.BlockSpec(wpb.shape, full),
            pl.BlockSpec(wpc.shape, full),
            pl.BlockSpec(wo.shape, full),
        ],
        out_specs=[pl.BlockSpec((None, ROW_BLOCK, d), row), pl.BlockSpec((None, ROW_BLOCK, d), row)],
        out_shape=[jax.ShapeDtypeStruct((nb, t, d), F32), jax.ShapeDtypeStruct((nb, t, d), BF16)],
        compiler_params=_params("parallel", "parallel"),
        name="merge",
    )(ya, yb, yc, u, u, u, x, mod, mod, g2.reshape(1, d), wpa, wpb, wpc, wo)


def _mlp_kernel(h_ref, x_ref, modb_ref, modc_ref, nmodb_ref, nmodc_ref, gn_ref, w1_ref, w2_ref,
                *out_refs, n_lat, ff_block, final):
    rows = x_ref.shape[0]
    is_ctx = _ctx_rows(pl.program_id(1), rows, n_lat)
    h = h_ref[...]
    acc = jnp.zeros(x_ref.shape, F32)
    for f in range(w1_ref.shape[1] // ff_block):
        cols = slice(f * ff_block, (f + 1) * ff_block)
        a = jnp.maximum(jnp.dot(h, w1_ref[:, cols], preferred_element_type=F32), 0.0)
        acc = acc + jnp.dot((a * a).astype(BF16), w2_ref[cols, :], preferred_element_type=F32)
    x = x_ref[...] + _mod(modb_ref, modc_ref, 5, is_ctx) * acc
    y = _rms(x) * gn_ref[...]
    if final:
        out_refs[0][...] = y
    else:
        out_refs[0][...] = x
        hn = y * (1.0 + _mod(nmodb_ref, nmodc_ref, 1, is_ctx)) + _mod(nmodb_ref, nmodc_ref, 0, is_ctx)
        out_refs[1][...] = hn.astype(out_refs[1].dtype)


def _mlp(h, x, mod, next_mod, gn, w1, w2, n_lat, final):
    nb, t, d = x.shape
    row = lambda b, j: (b, j, 0)
    full = lambda b, j: (0, 0)
    block = pl.BlockSpec((None, ROW_BLOCK, d), row)
    if final:
        out_specs = [block]
        out_shape = [jax.ShapeDtypeStruct((nb, t, d), F32)]
    else:
        out_specs = [block, block]
        out_shape = [jax.ShapeDtypeStruct((nb, t, d), F32), jax.ShapeDtypeStruct((nb, t, d), BF16)]
    return pl.pallas_call(
        functools.partial(_mlp_kernel, n_lat=n_lat, ff_block=1024, final=final),
        grid=(nb, t // ROW_BLOCK),
        in_specs=[block, block] + _mod_specs(d, nb) + _mod_specs(d, nb) + [
            pl.BlockSpec((1, d), full),
            pl.BlockSpec(w1.shape, full, pipeline_mode=pl.Buffered(1)),
            pl.BlockSpec(w2.shape, full, pipeline_mode=pl.Buffered(1)),
        ],
        out_specs=out_specs,
        out_shape=out_shape,
        compiler_params=_params("parallel", "parallel"),
        name="mlp",
    )(h, x, mod, mod, next_mod, next_mod, gn.reshape(1, d), w1, w2)


def _rope_tables(n_lat, n_ctx):
    rows = n_lat // GRID_W
    row = jnp.repeat(jnp.arange(rows), GRID_W).astype(F32)
    col = jnp.tile(jnp.arange(GRID_W), rows).astype(F32)

    def cos_sin(dim):
        n = dim // 4
        inv = ROPE_THETA ** (-jnp.arange(n, dtype=F32) / n)
        ang = jnp.concatenate([row[:, None] * inv, col[:, None] * inv], axis=-1)
        return jnp.cos(ang), jnp.sin(ang)

    def with_ctx(tab, fill):
        return jnp.concatenate([tab, jnp.full((n_ctx, tab.shape[1]), fill, F32)], axis=0)

    cos_h, sin_h = cos_sin(A_HD)
    cos_r, sin_r = cos_sin(C_DK)
    return (
        with_ctx(jnp.concatenate([cos_h, cos_h], axis=-1), 1.0),
        with_ctx(jnp.concatenate([-sin_h, sin_h], axis=-1), 0.0),
        with_ctx(cos_r, 1.0),
        with_ctx(sin_r, 0.0),
    )


def kernel(x, c, ctx, c_ctx, w_ada, b_ada, norm1_g, norm2_g, w_in, qn_g, kn_g, sink, ret_logit, ret_norm_g,
           w_pa, w_pb, w_pc, w_o, w_ff1, w_ff2, final_g):
    nb, n_lat, d = x.shape
    n_ctx = ctx.shape[1]
    depth = w_ada.shape[0]
    t = n_lat + n_ctx
    assert t % ROW_BLOCK == 0 and n_lat % TQ == 0 and n_ctx % TQ == 0 and n_lat % GRID_W == 0
    assert n_lat >= TQ + 2 * WINDOW and w_in.shape[2] == N_UNITS * UNIT

    tabs = _rope_tables(n_lat, n_ctx)
    mod_rows = 16
    craw = jnp.concatenate([c, c_ctx[None], jnp.zeros((mod_rows - nb - 1, d), F32)], axis=0)
    mod = _ada(craw, w_ada, b_ada).reshape(depth, mod_rows, N_MOD, d)
    perm = jnp.asarray(_UNIT_PERM, jnp.int32)

    xs = jnp.concatenate([x, ctx], axis=1)
    h = _prenorm(xs, mod[0], norm1_g[0], n_lat)
    for l in range(depth):
        w_in_l = w_in[l].reshape(d, N_UNITS, UNIT)[:, perm, :].reshape(d, N_UNITS * UNIT).astype(BF16)
        u = _inproj(h, w_in_l, tabs, qn_g[l], kn_g[l])
        ya = _attn_a(u, n_lat)
        yb = _attn_b(sink[l], u, n_lat)
        yc = _retention(ret_logit[l], u, ret_norm_g[l], n_lat)
        xs, h2 = _merge(ya, yb, yc, u, xs, mod[l], norm2_g[l], w_pa[l].astype(BF16), w_pb[l].astype(BF16),
                        w_pc[l].astype(BF16), w_o[l].astype(BF16), n_lat)
        final = l == depth - 1
        gn = final_g if final else norm1_g[l + 1]
        outs = _mlp(h2, xs, mod[l], mod[l if final else l + 1], gn, w_ff1[l].astype(BF16),
                    w_ff2[l].astype(BF16), n_lat, final)
        if final:
            return outs[0][:, :n_lat]
        xs, h = outs
```

```python
import functools
import math

import jax
import jax.numpy as jnp
from jax import lax
from jax.experimental import pallas as pl
from jax.experimental.pallas import tpu as pltpu

F32 = jnp.float32
BF16 = jnp.bfloat16

GRID_W = 64
WINDOW = 128
ROPE_THETA = 10000.0
EPS = 1e-6
NEG = -1e30
A_HEADS, A_KV, A_HD = 8, 2, 128
B_HEADS, B_KV, B_HD = 8, 2, 128
C_HEADS, C_DK, C_DV = 4, 256, 512
N_MOD = 6
LOG2E = math.log2(math.e)

LANES = 128
MXU_DIM = 256
V7X_VMEM_BYTES = 64 * 1024 * 1024
VMEM_LIMIT = V7X_VMEM_BYTES - 8 * 1024 * 1024

ROW_BLOCK = 768
TQ = MXU_DIM
RCH = MXU_DIM
UNIT = LANES

U_AQ, U_BQ = 0, 8
U_AK, U_BK, U_AV, U_BV = 16, 18, 20, 22
U_CQ, U_CK, U_CV, U_CG = 24, 32, 40, 56
U_GA, U_GB, U_GC = 72, 80, 88
N_UNITS = 96
_UNIT_PERM = (list(range(0, 8)) + list(range(12, 20)) + [8, 9, 20, 21, 10, 11, 22, 23]
              + list(range(24, 96)))
IN_BLOCK_UNITS = 8
IN_PAIR = MXU_DIM // UNIT


def _params(*sem):
    return pltpu.CompilerParams(dimension_semantics=sem, vmem_limit_bytes=VMEM_LIMIT)


def _rms(x):
    return x * lax.rsqrt(jnp.mean(x * x, axis=-1, keepdims=True) + EPS)


def _sigmoid(x):
    return 1.0 / (1.0 + jnp.exp(-x))


def _ctx_rows(j, rows, n_lat):
    r = j * rows + lax.broadcasted_iota(jnp.int32, (rows, 1), 0)
    return r >= n_lat


def _mod(modb_ref, modc_ref, idx, is_ctx):
    return jnp.where(is_ctx, modc_ref[idx:idx + 1, :], modb_ref[idx:idx + 1, :])


def _ada_kernel(c_ref, w_ref, b_ref, o_ref):
    c = c_ref[...]
    s = (c * _sigmoid(c)).astype(BF16)
    o_ref[...] = jnp.dot(s, w_ref[...].astype(BF16), preferred_element_type=F32) + b_ref[...]


def _ada(craw, w_ada, b_ada):
    depth, d, width = w_ada.shape
    rows = craw.shape[0]
    tn = 1536
    return pl.pallas_call(
        _ada_kernel,
        grid=(depth, width // tn),
        in_specs=[
            pl.BlockSpec((rows, d), lambda l, j: (0, 0)),
            pl.BlockSpec((None, d, tn), lambda l, j: (l, 0, j)),
            pl.BlockSpec((None, 1, tn), lambda l, j: (l, 0, j)),
        ],
        out_specs=pl.BlockSpec((None, rows, tn), lambda l, j: (l, 0, j)),
        out_shape=jax.ShapeDtypeStruct((depth, rows, width), F32),
        compiler_params=_params("parallel", "parallel"),
        name="ada",
    )(craw, w_ada, b_ada.reshape(depth, 1, width))


def _prenorm_kernel(x_ref, modb_ref, modc_ref, g_ref, h_ref, *, n_lat):
    rows = x_ref.shape[0]
    is_ctx = _ctx_rows(pl.program_id(1), rows, n_lat)
    y = _rms(x_ref[...]) * g_ref[...]
    h = y * (1.0 + _mod(modb_ref, modc_ref, 1, is_ctx)) + _mod(modb_ref, modc_ref, 0, is_ctx)
    h_ref[...] = h.astype(h_ref.dtype)


def _mod_specs(d, n_batch):
    return [
        pl.BlockSpec((None, N_MOD, d), lambda b, j: (b, 0, 0)),
        pl.BlockSpec((None, N_MOD, d), lambda b, j: (n_batch, 0, 0)),
    ]


def _prenorm(x, mod, g, n_lat):
    nb, t, d = x.shape
    row = lambda b, j: (b, j, 0)
    return pl.pallas_call(
        functools.partial(_prenorm_kernel, n_lat=n_lat),
        grid=(nb, t // ROW_BLOCK),
        in_specs=[pl.BlockSpec((None, ROW_BLOCK, d), row)] + _mod_specs(d, nb)
        + [pl.BlockSpec((1, d), lambda b, j: (0, 0))],
        out_specs=pl.BlockSpec((None, ROW_BLOCK, d), row),
        out_shape=jax.ShapeDtypeStruct((nb, t, d), BF16),
        compiler_params=_params("parallel", "parallel"),
        name="prenorm",
    )(x, mod, mod, g.reshape(1, d))


def _rope128(a, c, s):
    return a * c + pltpu.roll(a, LANES // 2, axis=1) * s


def _rms_heads(a):
    r = lax.broadcasted_iota(jnp.int32, (MXU_DIM, MXU_DIM), 0) // LANES
    c = lax.broadcasted_iota(jnp.int32, (MXU_DIM, MXU_DIM), 1) // LANES
    ones = (r == c).astype(BF16)
    sq = a * a
    hi = sq.astype(BF16)
    lo = (sq - hi.astype(F32)).astype(BF16)
    tot = (jnp.dot(hi, ones, preferred_element_type=F32)
           + jnp.dot(lo, ones, preferred_element_type=F32))
    return a * lax.rsqrt(tot * (1.0 / LANES) + EPS)


def _inproj_kernel(h_ref, w_ref, c128_ref, s128_ref, c256_ref, s256_ref, qn_ref, kn_ref, o_ref):
    j = pl.program_id(1)
    t = h_ref.shape[0]

    def for_pairs(fn):
        for ci in range(t // ROW_BLOCK):
            rows = pl.ds(ci * ROW_BLOCK, ROW_BLOCK)
            for u in range(0, IN_BLOCK_UNITS, IN_PAIR):
                cols = slice(u * UNIT, (u + IN_PAIR) * UNIT)
                a = jnp.dot(h_ref[rows, :], w_ref[:, cols], preferred_element_type=F32)
                o_ref[rows, cols] = fn(u, a, rows).astype(o_ref.dtype)

    def rope_heads(a, rows, gain=None, scale=None):
        out = []
        for a1 in (a[:, :UNIT], a[:, UNIT:]):
            if gain is not None:
                a1 = a1 * gain
            a1 = _rope128(a1, c128_ref[rows, :], s128_ref[rows, :])
            out.append(a1 if scale is None else a1 * scale)
        return jnp.concatenate(out, axis=1)

    @pl.when(j == U_AQ // IN_BLOCK_UNITS)
    def _():
        scale = A_HD ** -0.5 * LOG2E
        for_pairs(lambda u, a, rows: rope_heads(_rms_heads(a), rows, qn_ref[...], scale))

    @pl.when(j == U_BQ // IN_BLOCK_UNITS)
    def _():
        scale = B_HD ** -0.5 * LOG2E
        for_pairs(lambda u, a, rows: rope_heads(a, rows, scale=scale))

    @pl.when(j == U_AK // IN_BLOCK_UNITS)
    def _():
        def fn(u, a, rows):
            if u < U_BK - U_AK:
                return rope_heads(_rms_heads(a), rows, kn_ref[...])
            if u < U_AV - U_AK:
                return rope_heads(a, rows)
            return a
        for_pairs(fn)

    def rope_r(scale):
        def fn(u, a, rows):
            a0, a1 = a[:, :UNIT], a[:, UNIT:]
            c, s = c256_ref[rows, :], s256_ref[rows, :]
            return jnp.concatenate([(a0 * c - a1 * s) * scale, (a0 * s + a1 * c) * scale], axis=1)
        return fn

    @pl.when(j == U_CQ // IN_BLOCK_UNITS)
    def _():
        for_pairs(rope_r(1.0))

    @pl.when(j == U_CK // IN_BLOCK_UNITS)
    def _():
        for_pairs(rope_r(C_DK ** -0.5))

    @pl.when(j >= U_CV // IN_BLOCK_UNITS)
    def _():
        for_pairs(lambda u, a, rows: a)


def _inproj(h, w, tabs, qn, kn):
    nb, t, d = h.shape
    width = w.shape[1]
    tn = IN_BLOCK_UNITS * UNIT
    tab = pl.BlockSpec((t, LANES), lambda b, j: (0, 0))
    gain = pl.BlockSpec((1, LANES), lambda b, j: (0, 0))
    return pl.pallas_call(
        _inproj_kernel,
        grid=(nb, width // tn),
        in_specs=[
            pl.BlockSpec((None, t, d), lambda b, j: (b, 0, 0)),
            pl.BlockSpec((d, tn), lambda b, j: (0, j)),
            tab, tab, tab, tab, gain, gain,
        ],
        out_specs=pl.BlockSpec((None, t, tn), lambda b, j: (b, 0, j)),
        out_shape=jax.ShapeDtypeStruct((nb, t, width), BF16),
        compiler_params=_params("parallel", "arbitrary"),
        name="inproj",
    )(h, w, *tabs, qn.reshape(1, LANES), kn.reshape(1, LANES))


def _scores_t(k, q):
    return lax.dot_general(k, q, (((1,), (1,)), ((), ())), preferred_element_type=F32)


def _attend_t(s_parts, vt_parts, sink=None):
    m = None
    for s in s_parts:
        mi = jnp.max(s, axis=0, keepdims=True)
        m = mi if m is None else jnp.maximum(m, mi)
    den = None
    if sink is not None:
        m = jnp.maximum(m, sink)
        den = jnp.exp2(sink - m)
    o = None
    for s, vt in zip(s_parts, vt_parts):
        p = jnp.exp2(s - m)
        li = jnp.sum(p, axis=0, keepdims=True)
        den = li if den is None else den + li
        oi = jnp.dot(vt, p.astype(BF16), preferred_element_type=F32)
        o = oi if o is None else o + oi
    return (o * (1.0 / den)).T


def _stack_heads(q_ref, group, hd):
    return jnp.concatenate([q_ref[:, g * hd:(g + 1) * hd] for g in range(group)], axis=0)


def _store_heads(o_ref, o, group, hd):
    for g in range(group):
        o_ref[:, g * hd:(g + 1) * hd] = o[g * TQ:(g + 1) * TQ, :].astype(o_ref.dtype)


def _attn_a_kernel(q_ref, k_ref, v_ref, o_ref, vt_scr, *, n_lat, group, hd):
    i = pl.program_id(2)

    @pl.when(i == 0)
    def _():
        vt_scr[...] = v_ref[...].T

    def run(k, vt):
        o = _attend_t([_scores_t(k, _stack_heads(q_ref, group, hd))], [vt])
        _store_heads(o_ref, o, group, hd)

    @pl.when(i < n_lat // TQ)
    def _():
        run(k_ref[...], vt_scr[...])

    @pl.when(i >= n_lat // TQ)
    def _():
        run(k_ref[n_lat:, :], vt_scr[:, n_lat:])


def _attn_a(u, n_lat):
    nb, t, _ = u.shape
    group = A_HEADS // A_KV
    qw = group * A_HD
    return pl.pallas_call(
        functools.partial(_attn_a_kernel, n_lat=n_lat, group=group, hd=A_HD),
        grid=(nb, A_KV, t // TQ),
        in_specs=[
            pl.BlockSpec((None, TQ, qw), lambda b, kv, i: (b, i, U_AQ * UNIT // qw + kv)),
            pl.BlockSpec((None, t, A_HD), lambda b, kv, i: (b, 0, U_AK + kv)),
            pl.BlockSpec((None, t, A_HD), lambda b, kv, i: (b, 0, U_AV + kv)),
        ],
        out_specs=pl.BlockSpec((None, TQ, qw), lambda b, kv, i: (b, i, kv)),
        out_shape=jax.ShapeDtypeStruct((nb, t, A_HEADS * A_HD), BF16),
        scratch_shapes=[pltpu.VMEM((A_HD, t), BF16)],
        compiler_params=_params("parallel", "parallel", "arbitrary"),
        name="attn_a",
    )(u, u, u)


def _attn_b_kernel(sink_ref, q_ref, k_ref, v_ref, o_ref, *, n_lat, group, hd):
    kv = pl.program_id(1)
    i = pl.program_id(2)
    span = TQ + 2 * WINDOW
    kc = k_ref[n_lat:, :]
    vtc = v_ref[n_lat:, :].T

    q = _stack_heads(q_ref, group, hd)
    sink = jnp.concatenate(
        [jnp.full((1, TQ), sink_ref[kv * group + g], F32) for g in range(group)], axis=1) * LOG2E

    @pl.when(i < n_lat // TQ)
    def _():
        start = pl.multiple_of(jnp.clip(i * TQ - WINDOW, 0, n_lat - span), WINDOW)
        kw = k_ref[pl.ds(start, span), :]
        vtw = v_ref[pl.ds(start, span), :].T
        kpos = start + lax.broadcasted_iota(jnp.int32, (span, group * TQ), 0)
        qpos = i * TQ + (lax.broadcasted_iota(jnp.int32, (span, group * TQ), 1) & (TQ - 1))
        s_w = jnp.where(jnp.abs(qpos - kpos) <= WINDOW, _scores_t(kw, q), NEG)
        o = _attend_t([_scores_t(kc, q), s_w], [vtc, vtw], sink)
        _store_heads(o_ref, o, group, hd)

    @pl.when(i >= n_lat // TQ)
    def _():
        o = _attend_t([_scores_t(kc, q)], [vtc], sink)
        _store_heads(o_ref, o, group, hd)


def _attn_b(sink, u, n_lat):
    nb, t, _ = u.shape
    group = B_HEADS // B_KV
    qw = group * B_HD
    return pl.pallas_call(
        functools.partial(_attn_b_kernel, n_lat=n_lat, group=group, hd=B_HD),
        grid=(nb, B_KV, t // TQ),
        in_specs=[
            pl.BlockSpec(memory_space=pltpu.SMEM),
            pl.BlockSpec((None, TQ, qw), lambda b, kv, i: (b, i, U_BQ * UNIT // qw + kv)),
            pl.BlockSpec((None, t, B_HD), lambda b, kv, i: (b, 0, U_BK + kv)),
            pl.BlockSpec((None, t, B_HD), lambda b, kv, i: (b, 0, U_BV + kv)),
        ],
        out_specs=pl.BlockSpec((None, TQ, qw), lambda b, kv, i: (b, i, kv)),
        out_shape=jax.ShapeDtypeStruct((nb, t, B_HEADS * B_HD), BF16),
        compiler_params=_params("parallel", "parallel", "arbitrary"),
        name="attn_b",
    )(sink, u, u, u)


def _scale_rows(x, v):
    return jnp.concatenate(
        [x[:, u * LANES:(u + 1) * LANES] * v for u in range(x.shape[1] // LANES)], axis=1)


def _retention_kernel(logit_ref, q_ref, k_ref, v_ref, g_ref, gn_ref, o_ref,
                      part_scr, state_scr, decay_scr, vec_scr, *, n_lat):
    h = pl.program_id(1)
    t = q_ref.shape[0]
    n_l, n_c = n_lat // RCH, (t - n_lat) // RCH
    n = n_l + n_c
    order = ([n_l + c for c in range(n_c)] + list(range(n_l)),
             [n_l + c for c in reversed(range(n_c))] + list(reversed(range(n_l))))
    when = tuple({c: i for i, c in enumerate(o)} for o in order)

    def log_gamma(direction, width):
        x = jnp.full((1, width), logit_ref[direction, h], F32)
        return jnp.minimum(x, 0.0) - jnp.log1p(jnp.exp(-jnp.abs(x)))

    qi = lax.broadcasted_iota(jnp.int32, (RCH, RCH), 0).astype(F32)
    si = lax.broadcasted_iota(jnp.int32, (RCH, RCH), 1).astype(F32)
    ri = lax.broadcasted_iota(jnp.int32, (RCH, LANES), 0).astype(F32)
    carry = []
    for d in range(2):
        lag = qi - si if d == 0 else si - qi
        decay_scr[d] = jnp.where(lag >= 0, jnp.exp(jnp.maximum(lag, 0.0) * log_gamma(d, RCH)), 0.0)
        lg = log_gamma(d, LANES)
        vec_scr[2 * d] = jnp.exp((ri + 1.0 if d == 0 else RCH - ri) * lg)
        vec_scr[2 * d + 1] = jnp.exp((RCH - 1.0 - ri if d == 0 else ri) * lg)
        carry.append(jnp.exp(RCH * log_gamma(d, C_DV)))

    def step(d, i):
        rows = pl.ds(order[d][i] * RCH, RCH)
        qb, kb, vb = q_ref[rows, :], k_ref[rows, :], v_ref[rows, :]
        att = _scores_t(qb, kb) * decay_scr[d]
        o = jnp.dot(att.astype(BF16), vb, preferred_element_type=F32)
        if i > 0:
            cross = jnp.dot(qb, state_scr[d].astype(BF16), preferred_element_type=F32)
            o = o + _scale_rows(cross, vec_scr[2 * d])
        if i < n - 1:
            kd = _scale_rows(kb.astype(F32), vec_scr[2 * d + 1]).astype(BF16)
            upd = lax.dot_general(kd, vb, (((0,), (0,)), ((), ())), preferred_element_type=F32)
            state_scr[d] = upd if i == 0 else state_scr[d] * carry[d] + upd
        return o

    def finish(c, o):
        rows = pl.ds(c * RCH, RCH)
        gate = g_ref[rows, :].astype(F32)
        y = _rms(o) * gn_ref[...] * (gate * _sigmoid(gate))
        o_ref[rows, :] = y.astype(o_ref.dtype)

    for i in range(n):
        outs = [step(d, i) for d in range(2)]
        if order[0][i] == order[1][i]:
            finish(order[0][i], outs[0] + outs[1])
            continue
        for d in range(2):
            c = order[d][i]
            rows = pl.ds(c * RCH, RCH)
            if when[1 - d][c] < i:
                finish(c, outs[d] + part_scr[rows, :])
            else:
                part_scr[rows, :] = outs[d]


def _retention(logit, u, gn, n_lat):
    nb, t, _ = u.shape
    return pl.pallas_call(
        functools.partial(_retention_kernel, n_lat=n_lat),
        grid=(nb, C_HEADS),
        in_specs=[
            pl.BlockSpec(memory_space=pltpu.SMEM),
            pl.BlockSpec((None, t, C_DK), lambda b, h: (b, 0, U_CQ * UNIT // C_DK + h)),
            pl.BlockSpec((None, t, C_DK), lambda b, h: (b, 0, U_CK * UNIT // C_DK + h)),
            pl.BlockSpec((None, t, C_DV), lambda b, h: (b, 0, U_CV * UNIT // C_DV + h)),
            pl.BlockSpec((None, t, C_DV), lambda b, h: (b, 0, U_CG * UNIT // C_DV + h)),
            pl.BlockSpec((1, C_DV), lambda b, h: (0, h)),
        ],
        out_specs=pl.BlockSpec((None, t, C_DV), lambda b, h: (b, 0, h)),
        out_shape=jax.ShapeDtypeStruct((nb, t, C_HEADS * C_DV), BF16),
        scratch_shapes=[
            pltpu.VMEM((t, C_DV), F32),
            pltpu.VMEM((2, C_DK, C_DV), F32),
            pltpu.VMEM((2, RCH, RCH), F32),
            pltpu.VMEM((4, RCH, LANES), F32),
        ],
        compiler_params=_params("parallel", "parallel"),
        name="retention",
    )(logit, u, u, u, u, gn.reshape(1, C_HEADS * C_DV))


def _merge_kernel(ya_ref, yb_ref, yc_ref, ga_ref, gb_ref, gc_ref, x_ref, modb_ref, modc_ref, g2_ref,
                  wpa_ref, wpb_ref, wpc_ref, wo_ref, x_out_ref, h_out_ref, *, n_lat):
    rows = x_ref.shape[0]
    is_ctx = _ctx_rows(pl.program_id(1), rows, n_lat)

    def branch(y_ref, w_ref, gate_ref):
        p = jnp.dot(y_ref[...], w_ref[...], preferred_element_type=F32)
        return _sigmoid(gate_ref[...].astype(F32)) * p

    y = branch(ya_ref, wpa_ref, ga_ref) + branch(yb_ref, wpb_ref, gb_ref) + branch(yc_ref, wpc_ref, gc_ref)
    o = jnp.dot(y.astype(BF16), wo_ref[...], preferred_element_type=F32)
    x = x_ref[...] + _mod(modb_ref, modc_ref, 2, is_ctx) * o
    x_out_ref[...] = x
    h = _rms(x) * g2_ref[...]
    h = h * (1.0 + _mod(modb_ref, modc_ref, 4, is_ctx)) + _mod(modb_ref, modc_ref, 3, is_ctx)
    h_out_ref[...] = h.astype(h_out_ref.dtype)


def _merge(ya, yb, yc, u, x, mod, g2, wpa, wpb, wpc, wo, n_lat):
    nb, t, d = x.shape
    row = lambda b, j: (b, j, 0)
    full = lambda b, j: (0, 0)
    gate = lambda unit: pl.BlockSpec((None, ROW_BLOCK, d), lambda b, j: (b, j, unit * UNIT // d))
    return pl.pallas_call(
        functools.partial(_merge_kernel, n_lat=n_lat),
        grid=(nb, t // ROW_BLOCK),
        in_specs=[
            pl.BlockSpec((None, ROW_BLOCK, ya.shape[2]), row),
            pl.BlockSpec((None, ROW_BLOCK, yb.shape[2]), row),
            pl.BlockSpec((None, ROW_BLOCK, yc.shape[2]), row),
            gate(U_GA), gate(U_GB), gate(U_GC),
            pl.BlockSpec((None, ROW_BLOCK, d), row),
        ] + _mod_specs(d, nb) + [
            pl.BlockSpec((1, d), full),
            pl.BlockSpec(wpa.shape, full),
            pl.BlockSpec(wpb.shape, full),
            pl.BlockSpec(wpc.shape, full),
            pl.BlockSpec(wo.shape, full),
        ],
        out_specs=[pl.BlockSpec((None, ROW_BLOCK, d), row), pl.BlockSpec((None, ROW_BLOCK, d), row)],
        out_shape=[jax.ShapeDtypeStruct((nb, t, d), F32), jax.ShapeDtypeStruct((nb, t, d), BF16)],
        compiler_params=_params("parallel", "parallel"),
        name="merge",
    )(ya, yb, yc, u, u, u, x, mod, mod, g2.reshape(1, d), wpa, wpb, wpc, wo)


def _mlp_kernel(h_ref, x_ref, modb_ref, modc_ref, nmodb_ref, nmodc_ref, gn_ref, w1_ref, w2_ref,
                *out_refs, n_lat, ff_block, final):
    rows = x_ref.shape[0]
    is_ctx = _ctx_rows(pl.program_id(1), rows, n_lat)
    h = h_ref[...]
    acc = jnp.zeros(x_ref.shape, F32)
    for f in range(w1_ref.shape[1] // ff_block):
        cols = slice(f * ff_block, (f + 1) * ff_block)
        a = jnp.maximum(jnp.dot(h, w1_ref[:, cols], preferred_element_type=F32), 0.0)
        acc = acc + jnp.dot((a * a).astype(BF16), w2_ref[cols, :], preferred_element_type=F32)
    x = x_ref[...] + _mod(modb_ref, modc_ref, 5, is_ctx) * acc
    y = _rms(x) * gn_ref[...]
    if final:
        out_refs[0][...] = y
    else:
        out_refs[0][...] = x
        hn = y * (1.0 + _mod(nmodb_ref, nmodc_ref, 1, is_ctx)) + _mod(nmodb_ref, nmodc_ref, 0, is_ctx)
        out_refs[1][...] = hn.astype(out_refs[1].dtype)


def _mlp(h, x, mod, next_mod, gn, w1, w2, n_lat, final):
    nb, t, d = x.shape
    row = lambda b, j: (b, j, 0)
    full = lambda b, j: (0, 0)
    block = pl.BlockSpec((None, ROW_BLOCK, d), row)
    if final:
        out_specs = [block]
        out_shape = [jax.ShapeDtypeStruct((nb, n_lat, d), F32)]
    else:
        out_specs = [block, block]
        out_shape = [jax.ShapeDtypeStruct((nb, t, d), F32), jax.ShapeDtypeStruct((nb, t, d), BF16)]
    return pl.pallas_call(
        functools.partial(_mlp_kernel, n_lat=n_lat, ff_block=1024, final=final),
        grid=(nb, t // ROW_BLOCK),
        in_specs=[block, block] + _mod_specs(d, nb) + _mod_specs(d, nb) + [
            pl.BlockSpec((1, d), full),
            pl.BlockSpec(w1.shape, full, pipeline_mode=pl.Buffered(1)),
            pl.BlockSpec(w2.shape, full, pipeline_mode=pl.Buffered(1)),
        ],
        out_specs=out_specs,
        out_shape=out_shape,
        compiler_params=_params("parallel", "parallel"),
        name="mlp",
    )(h, x, mod, mod, next_mod, next_mod, gn.reshape(1, d), w1, w2)


def _rope_tables(n_lat, n_ctx):
    rows = n_lat // GRID_W
    row = jnp.repeat(jnp.arange(rows), GRID_W).astype(F32)
    col = jnp.tile(jnp.arange(GRID_W), rows).astype(F32)

    def cos_sin(dim):
        n = dim // 4
        inv = ROPE_THETA ** (-jnp.arange(n, dtype=F32) / n)
        ang = jnp.concatenate([row[:, None] * inv, col[:, None] * inv], axis=-1)
        return jnp.cos(ang), jnp.sin(ang)

    def with_ctx(tab, fill):
        return jnp.concatenate([tab, jnp.full((n_ctx, tab.shape[1]), fill, F32)], axis=0)

    cos_h, sin_h = cos_sin(A_HD)
    cos_r, sin_r = cos_sin(C_DK)
    return (
        with_ctx(jnp.concatenate([cos_h, cos_h], axis=-1), 1.0),
        with_ctx(jnp.concatenate([-sin_h, sin_h], axis=-1), 0.0),
        with_ctx(cos_r, 1.0),
        with_ctx(sin_r, 0.0),
    )


def kernel(x, c, ctx, c_ctx, w_ada, b_ada, norm1_g, norm2_g, w_in, qn_g, kn_g, sink, ret_logit, ret_norm_g,
           w_pa, w_pb, w_pc, w_o, w_ff1, w_ff2, final_g):
    nb, n_lat, d = x.shape
    n_ctx = ctx.shape[1]
    depth = w_ada.shape[0]
    t = n_lat + n_ctx
    assert t % ROW_BLOCK == 0 and n_lat % TQ == 0 and n_ctx % TQ == 0 and n_lat % GRID_W == 0
    assert n_lat >= TQ + 2 * WINDOW and w_in.shape[2] == N_UNITS * UNIT

    tabs = _rope_tables(n_lat, n_ctx)
    mod_rows = 16
    craw = jnp.concatenate([c, c_ctx[None], jnp.zeros((mod_rows - nb - 1, d), F32)], axis=0)
    mod = _ada(craw, w_ada, b_ada).reshape(depth, mod_rows, N_MOD, d)
    perm = jnp.asarray(_UNIT_PERM, jnp.int32)

    xs = jnp.concatenate([x, ctx], axis=1)
    h = _prenorm(xs, mod[0], norm1_g[0], n_lat)
    for l in range(depth):
        w_in_l = w_in[l].reshape(d, N_UNITS, UNIT)[:, perm, :].reshape(d, N_UNITS * UNIT).astype(BF16)
        u = _inproj(h, w_in_l, tabs, qn_g[l], kn_g[l])
        ya = _attn_a(u, n_lat)
        yb = _attn_b(sink[l], u, n_lat)
        yc = _retention(ret_logit[l], u, ret_norm_g[l], n_lat)
        xs, h2 = _merge(ya, yb, yc, u, xs, mod[l], norm2_g[l], w_pa[l].astype(BF16), w_pb[l].astype(BF16),
                        w_pc[l].astype(BF16), w_o[l].astype(BF16), n_lat)
        final = l == depth - 1
        gn = final_g if final else norm1_g[l + 1]
        outs = _mlp(h2, xs, mod[l], mod[l if final else l + 1], gn, w_ff1[l].astype(BF16),
                    w_ff2[l].astype(BF16), n_lat, final)
        if final:
            return outs[0]
        xs, h = outs
```

```python
import functools
import math

import jax
import jax.numpy as jnp
from jax import lax
from jax.experimental import pallas as pl
from jax.experimental.pallas import tpu as pltpu

F32 = jnp.float32
BF16 = jnp.bfloat16

GRID_W = 64
WINDOW = 128
ROPE_THETA = 10000.0
EPS = 1e-6
NEG = -1e30
A_HEADS, A_KV, A_HD = 8, 2, 128
B_HEADS, B_KV, B_HD = 8, 2, 128
C_HEADS, C_DK, C_DV = 4, 256, 512
N_MOD = 6
LOG2E = math.log2(math.e)

LANES = 128
MXU_DIM = 256
V7X_VMEM_BYTES = 64 * 1024 * 1024
VMEM_LIMIT = V7X_VMEM_BYTES - 8 * 1024 * 1024

ROW_BLOCK = 768
TQ = MXU_DIM
KCH = MXU_DIM
RCH = MXU_DIM
UNIT = LANES

U_AQ, U_BQ = 0, 8
U_AK, U_BK, U_AV, U_BV = 16, 18, 20, 22
U_CQ, U_CK, U_CV, U_CG = 24, 32, 40, 56
U_GA, U_GB, U_GC = 72, 80, 88
N_UNITS = 96
_UNIT_PERM = (list(range(0, 8)) + list(range(12, 20)) + [8, 9, 20, 21, 10, 11, 22, 23]
              + list(range(24, 96)))
IN_BLOCK_UNITS = 8
IN_PAIR = MXU_DIM // UNIT


def _params(*sem):
    return pltpu.CompilerParams(dimension_semantics=sem, vmem_limit_bytes=VMEM_LIMIT)


def _rms(x):
    return x * lax.rsqrt(jnp.mean(x * x, axis=-1, keepdims=True) + EPS)


def _sigmoid(x):
    return 1.0 / (1.0 + jnp.exp(-x))


def _ctx_rows(j, rows, n_lat):
    r = j * rows + lax.broadcasted_iota(jnp.int32, (rows, 1), 0)
    return r >= n_lat


def _mod(modb_ref, modc_ref, idx, is_ctx):
    return jnp.where(is_ctx, modc_ref[idx:idx + 1, :], modb_ref[idx:idx + 1, :])


def _ada_kernel(c_ref, w_ref, b_ref, o_ref):
    c = c_ref[...]
    s = (c * _sigmoid(c)).astype(BF16)
    o_ref[...] = jnp.dot(s, w_ref[...].astype(BF16), preferred_element_type=F32) + b_ref[...]


def _ada(craw, w_ada, b_ada):
    depth, d, width = w_ada.shape
    rows = craw.shape[0]
    tn = 1536
    return pl.pallas_call(
        _ada_kernel,
        grid=(depth, width // tn),
        in_specs=[
            pl.BlockSpec((rows, d), lambda l, j: (0, 0)),
            pl.BlockSpec((None, d, tn), lambda l, j: (l, 0, j)),
            pl.BlockSpec((None, 1, tn), lambda l, j: (l, 0, j)),
        ],
        out_specs=pl.BlockSpec((None, rows, tn), lambda l, j: (l, 0, j)),
        out_shape=jax.ShapeDtypeStruct((depth, rows, width), F32),
        compiler_params=_params("parallel", "parallel"),
        name="ada",
    )(craw, w_ada, b_ada.reshape(depth, 1, width))


def _prenorm_kernel(x_ref, ctx_ref, modb_ref, modc_ref, g_ref, xs_ref, h_ref, *, n_lat):
    rows = x_ref.shape[0]
    j = pl.program_id(1)
    is_ctx = _ctx_rows(j, rows, n_lat)
    x = jnp.where(j * rows < n_lat, x_ref[...], ctx_ref[...])
    xs_ref[...] = x
    y = _rms(x) * g_ref[...]
    h = y * (1.0 + _mod(modb_ref, modc_ref, 1, is_ctx)) + _mod(modb_ref, modc_ref, 0, is_ctx)
    h_ref[...] = h.astype(h_ref.dtype)


def _mod_specs(d, n_batch):
    return [
        pl.BlockSpec((None, N_MOD, d), lambda b, j: (b, 0, 0)),
        pl.BlockSpec((None, N_MOD, d), lambda b, j: (n_batch, 0, 0)),
    ]


def _prenorm(x, ctx, mod, g):
    nb, n_lat, d = x.shape
    n_ctx = ctx.shape[1]
    rows = math.gcd(n_lat, n_ctx)
    n_lat_blocks = n_lat // rows
    row = lambda b, j: (b, j, 0)
    block = pl.BlockSpec((None, rows, d), row)
    return pl.pallas_call(
        functools.partial(_prenorm_kernel, n_lat=n_lat),
        grid=(nb, (n_lat + n_ctx) // rows),
        in_specs=[
            pl.BlockSpec((None, rows, d), lambda b, j: (b, jnp.minimum(j, n_lat_blocks - 1), 0)),
            pl.BlockSpec((None, rows, d), lambda b, j: (b, jnp.maximum(j - n_lat_blocks, 0), 0)),
        ] + _mod_specs(d, nb) + [pl.BlockSpec((1, d), lambda b, j: (0, 0))],
        out_specs=[block, block],
        out_shape=[jax.ShapeDtypeStruct((nb, n_lat + n_ctx, d), F32),
                   jax.ShapeDtypeStruct((nb, n_lat + n_ctx, d), BF16)],
        compiler_params=_params("parallel", "parallel"),
        name="prenorm",
    )(x, ctx, mod, mod, g.reshape(1, d))


def _rope128(a, c, s):
    return a * c + pltpu.roll(a, LANES // 2, axis=1) * s


def _rms_heads(a):
    r = lax.broadcasted_iota(jnp.int32, (MXU_DIM, MXU_DIM), 0) // LANES
    c = lax.broadcasted_iota(jnp.int32, (MXU_DIM, MXU_DIM), 1) // LANES
    ones = (r == c).astype(BF16)
    sq = a * a
    hi = sq.astype(BF16)
    lo = (sq - hi.astype(F32)).astype(BF16)
    tot = (jnp.dot(hi, ones, preferred_element_type=F32)
           + jnp.dot(lo, ones, preferred_element_type=F32))
    return a * lax.rsqrt(tot * (1.0 / LANES) + EPS)


def _inproj_kernel(src_ref, h_ref, *refs):
    del src_ref
    n_pairs = IN_BLOCK_UNITS // IN_PAIR
    w_refs = refs[:n_pairs]
    c128_ref, s128_ref, c256_ref, s256_ref, qn_ref, kn_ref, o_ref, w_ref = refs[n_pairs:]
    j = pl.program_id(0)
    t = h_ref.shape[0]

    @pl.when(pl.program_id(1) == 0)
    def _():
        for p, wp_ref in enumerate(w_refs):
            w_ref[:, p * MXU_DIM:(p + 1) * MXU_DIM] = wp_ref[...].astype(w_ref.dtype)

    def for_pairs(fn):
        steps = [(pl.ds(ci * ROW_BLOCK, ROW_BLOCK), u)
                 for ci in range(t // ROW_BLOCK) for u in range(0, IN_BLOCK_UNITS, IN_PAIR)]
        pending = None
        for rows, u in steps + [(None, None)]:
            if rows is not None:
                cols = slice(u * UNIT, (u + IN_PAIR) * UNIT)
                a = jnp.dot(h_ref[rows, :], w_ref[:, cols], preferred_element_type=F32)
            if pending is not None:
                p_rows, p_u, p_a = pending
                o_ref[p_rows, p_u * UNIT:(p_u + IN_PAIR) * UNIT] = fn(p_u, p_a, p_rows).astype(o_ref.dtype)
            pending = (rows, u, a) if rows is not None else None

    def rope_heads(a, rows, gain=None, scale=None):
        out = []
        for a1 in (a[:, :UNIT], a[:, UNIT:]):
            if gain is not None:
                a1 = a1 * gain
            a1 = _rope128(a1, c128_ref[rows, :], s128_ref[rows, :])
            out.append(a1 if scale is None else a1 * scale)
        return jnp.concatenate(out, axis=1)

    @pl.when(j == U_AQ // IN_BLOCK_UNITS)
    def _():
        scale = A_HD ** -0.5 * LOG2E
        for_pairs(lambda u, a, rows: rope_heads(_rms_heads(a), rows, qn_ref[...], scale))

    @pl.when(j == U_BQ // IN_BLOCK_UNITS)
    def _():
        scale = B_HD ** -0.5 * LOG2E
        for_pairs(lambda u, a, rows: rope_heads(a, rows, scale=scale))

    @pl.when(j == U_AK // IN_BLOCK_UNITS)
    def _():
        def fn(u, a, rows):
            if u < U_BK - U_AK:
                return rope_heads(_rms_heads(a), rows, kn_ref[...])
            if u < U_AV - U_AK:
                return rope_heads(a, rows)
            return a
        for_pairs(fn)

    def rope_r(scale):
        def fn(u, a, rows):
            a0, a1 = a[:, :UNIT], a[:, UNIT:]
            c, s = c256_ref[rows, :], s256_ref[rows, :]
            return jnp.concatenate([(a0 * c - a1 * s) * scale, (a0 * s + a1 * c) * scale], axis=1)
        return fn

    @pl.when(j == U_CQ // IN_BLOCK_UNITS)
    def _():
        for_pairs(rope_r(1.0))

    @pl.when(j == U_CK // IN_BLOCK_UNITS)
    def _():
        for_pairs(rope_r(C_DK ** -0.5))

    @pl.when(j >= U_CV // IN_BLOCK_UNITS)
    def _():
        for_pairs(lambda u, a, rows: a)


def _inproj(h, w_in, layer, tabs, qn, kn):
    nb, t, d = h.shape
    width = w_in.shape[2]
    tn = IN_BLOCK_UNITS * UNIT
    n_pairs = IN_BLOCK_UNITS // IN_PAIR
    assert all(a % IN_PAIR == 0 and b == a + 1 for a, b in zip(_UNIT_PERM[0::2], _UNIT_PERM[1::2]))
    src = jnp.asarray([a // IN_PAIR for a in _UNIT_PERM[0::IN_PAIR]], jnp.int32)
    tab = pl.BlockSpec((t, LANES), lambda j, b, src: (0, 0))
    gain = pl.BlockSpec((1, LANES), lambda j, b, src: (0, 0))

    def pair(p):
        return pl.BlockSpec((None, d, MXU_DIM), lambda j, b, src: (layer, 0, src[j * n_pairs + p]))

    return pl.pallas_call(
        _inproj_kernel,
        grid_spec=pltpu.PrefetchScalarGridSpec(
            num_scalar_prefetch=1,
            grid=(width // tn, nb),
            in_specs=[pl.BlockSpec((None, t, d), lambda j, b, src: (b, 0, 0))]
            + [pair(p) for p in range(n_pairs)] + [tab, tab, tab, tab, gain, gain],
            out_specs=pl.BlockSpec((None, t, tn), lambda j, b, src: (b, 0, j)),
            scratch_shapes=[pltpu.VMEM((d, tn), BF16)],
        ),
        out_shape=jax.ShapeDtypeStruct((nb, t, width), BF16),
        compiler_params=_params("arbitrary", "arbitrary"),
        name="inproj",
    )(src, h, *([w_in] * n_pairs), *tabs, qn.reshape(1, LANES), kn.reshape(1, LANES))


def _scores_t(k, q):
    return lax.dot_general(k, q, (((1,), (1,)), ((), ())), preferred_element_type=F32)


def _attend_t(s_parts, vt_parts, sink=None):
    m = None
    for s in s_parts:
        mi = jnp.max(s, axis=0, keepdims=True)
        m = mi if m is None else jnp.maximum(m, mi)
    den = None
    if sink is not None:
        m = jnp.maximum(m, sink)
        den = jnp.exp2(sink - m)
    o = None
    for s, vt in zip(s_parts, vt_parts):
        p = jnp.exp2(s - m)
        li = jnp.sum(p, axis=0, keepdims=True)
        den = li if den is None else den + li
        oi = jnp.dot(vt, p.astype(BF16), preferred_element_type=F32)
        o = oi if o is None else o + oi
    return (o * (1.0 / den)).T


def _stack_heads(q_ref, group, hd):
    return jnp.concatenate([q_ref[:, g * hd:(g + 1) * hd] for g in range(group)], axis=0)


def _store_heads(o_ref, o, group, hd):
    for g in range(group):
        o_ref[:, g * hd:(g + 1) * hd] = o[g * TQ:(g + 1) * TQ, :].astype(o_ref.dtype)


def _attn_a_kernel(q_ref, k_ref, v_ref, o_ref, vt_scr, s_scr, m_scr, *, n_lat, group, hd):
    i = pl.program_id(2)
    t = k_ref.shape[0]
    n_q_lat = n_lat // TQ
    all_chunks = range(t // KCH)
    ctx_chunks = range(n_lat // KCH, t // KCH)

    @pl.when(i == 0)
    def _():
        vt_scr[...] = v_ref[...].T

    def rows(c):
        return pl.ds(c * KCH, KCH)

    def score_chunk(q, c, m):
        s = _scores_t(k_ref[rows(c), :], q)
        s_scr[rows(c), :] = s
        mc = jnp.max(s, axis=0, keepdims=True)
        return mc if m is None else jnp.maximum(m, mc)

    def value_chunk(c, m, den, o):
        p = jnp.exp2(s_scr[rows(c), :] - m)
        dc = jnp.sum(p, axis=0, keepdims=True)
        oc = jnp.dot(vt_scr[:, rows(c)], p.astype(BF16), preferred_element_type=F32)
        return (dc, oc) if den is None else (den + dc, o + oc)

    def run(score_chunks, value_chunks):
        q = _stack_heads(q_ref, group, hd) if score_chunks else None
        m_old = m_scr[0:1, :] if value_chunks else None
        m_new = den = o = None
        for c in sorted(set(score_chunks) | set(value_chunks)):
            if c in value_chunks:
                den, o = value_chunk(c, m_old, den, o)
            if c in score_chunks:
                m_new = score_chunk(q, c, m_new)
        if value_chunks:
            _store_heads(o_ref, (o * (1.0 / den)).T, group, hd)
        if score_chunks:
            m_scr[0:1, :] = m_new

    @pl.when(i == 0)
    def _():
        run(all_chunks, ())

    @pl.when((i > 0) & (i < n_q_lat))
    def _():
        run(all_chunks, all_chunks)

    @pl.when(i == n_q_lat)
    def _():
        run(ctx_chunks, all_chunks)

    @pl.when(i > n_q_lat)
    def _():
        run((), ctx_chunks)


def _attn_a(u, n_lat):
    nb, t, _ = u.shape
    group = A_HEADS // A_KV
    qw = group * A_HD
    n_q = t // TQ
    assert n_q == n_lat // TQ + 1, "one context query block expected"
    return pl.pallas_call(
        functools.partial(_attn_a_kernel, n_lat=n_lat, group=group, hd=A_HD),
        grid=(nb, A_KV, n_q + 1),
        in_specs=[
            pl.BlockSpec((None, TQ, qw),
                         lambda b, kv, i: (b, jnp.minimum(i, n_q - 1), U_AQ * UNIT // qw + kv)),
            pl.BlockSpec((None, t, A_HD), lambda b, kv, i: (b, 0, U_AK + kv)),
            pl.BlockSpec((None, t, A_HD), lambda b, kv, i: (b, 0, U_AV + kv)),
        ],
        out_specs=pl.BlockSpec((None, TQ, qw), lambda b, kv, i: (b, jnp.maximum(i - 1, 0), kv)),
        out_shape=jax.ShapeDtypeStruct((nb, t, A_HEADS * A_HD), BF16),
        scratch_shapes=[
            pltpu.VMEM((A_HD, t), BF16),
            pltpu.VMEM((t, group * TQ), F32),
            pltpu.VMEM((8, group * TQ), F32),
        ],
        compiler_params=_params("parallel", "parallel", "arbitrary"),
        name="attn_a",
    )(u, u, u)


def _attn_b_kernel(sink_ref, q_ref, k_ref, v_ref, o_ref, band_scr, *, n_lat, group, hd):
    kv = pl.program_id(1)
    i = pl.program_id(2)
    span = TQ + 2 * WINDOW
    kc = k_ref[n_lat:, :]
    vtc = v_ref[n_lat:, :].T

    q = _stack_heads(q_ref, group, hd)
    sink = jnp.concatenate(
        [jnp.full((1, TQ), sink_ref[kv * group + g], F32) for g in range(group)], axis=1) * LOG2E

    @pl.when(i == 0)
    def _():
        r = lax.broadcasted_iota(jnp.int32, (span, TQ), 0)
        c = lax.broadcasted_iota(jnp.int32, (span, TQ), 1)
        for lead in range(band_scr.shape[0]):
            band_scr[lead] = jnp.where(jnp.abs(c + lead * WINDOW - r) <= WINDOW, 0.0, NEG)

    @pl.when(i < n_lat // TQ)
    def _():
        start = pl.multiple_of(jnp.clip(i * TQ - WINDOW, 0, n_lat - span), WINDOW)
        kw = k_ref[pl.ds(start, span), :]
        vtw = v_ref[pl.ds(start, span), :].T
        band = band_scr[(i * TQ - start) // WINDOW]
        s_w = _scores_t(kw, q)
        s_w = jnp.concatenate([s_w[:, g * TQ:(g + 1) * TQ] + band for g in range(group)], axis=1)
        o = _attend_t([_scores_t(kc, q), s_w], [vtc, vtw], sink)
        _store_heads(o_ref, o, group, hd)

    @pl.when(i >= n_lat // TQ)
    def _():
        o = _attend_t([_scores_t(kc, q)], [vtc], sink)
        _store_heads(o_ref, o, group, hd)


def _attn_b(sink, u, n_lat):
    nb, t, _ = u.shape
    group = B_HEADS // B_KV
    qw = group * B_HD
    return pl.pallas_call(
        functools.partial(_attn_b_kernel, n_lat=n_lat, group=group, hd=B_HD),
        grid=(nb, B_KV, t // TQ),
        in_specs=[
            pl.BlockSpec(memory_space=pltpu.SMEM),
            pl.BlockSpec((None, TQ, qw), lambda b, kv, i: (b, i, U_BQ * UNIT // qw + kv)),
            pl.BlockSpec((None, t, B_HD), lambda b, kv, i: (b, 0, U_BK + kv)),
            pl.BlockSpec((None, t, B_HD), lambda b, kv, i: (b, 0, U_BV + kv)),
        ],
        out_specs=pl.BlockSpec((None, TQ, qw), lambda b, kv, i: (b, i, kv)),
        out_shape=jax.ShapeDtypeStruct((nb, t, B_HEADS * B_HD), BF16),
        scratch_shapes=[pltpu.VMEM((3, TQ + 2 * WINDOW, TQ), F32)],
        compiler_params=_params("parallel", "parallel", "arbitrary"),
        name="attn_b",
    )(sink, u, u, u)


def _scale_rows(x, v):
    return jnp.concatenate(
        [x[:, u * LANES:(u + 1) * LANES] * v for u in range(x.shape[1] // LANES)], axis=1)


def _retention_kernel(logit_ref, q_ref, k_ref, v_ref, g_ref, gn_ref, o_ref,
                      part_scr, state_scr, decay_scr, vec_scr, *, n_lat):
    h = pl.program_id(1)
    t = q_ref.shape[0]
    n_l, n_c = n_lat // RCH, (t - n_lat) // RCH
    n = n_l + n_c
    order = ([n_l + c for c in range(n_c)] + list(range(n_l)),
             [n_l + c for c in reversed(range(n_c))] + list(reversed(range(n_l))))
    when = tuple({c: i for i, c in enumerate(o)} for o in order)

    def log_gamma(direction, width):
        x = jnp.full((1, width), logit_ref[direction, h], F32)
        return jnp.minimum(x, 0.0) - jnp.log1p(jnp.exp(-jnp.abs(x)))

    qi = lax.broadcasted_iota(jnp.int32, (RCH, RCH), 0).astype(F32)
    si = lax.broadcasted_iota(jnp.int32, (RCH, RCH), 1).astype(F32)
    ri = lax.broadcasted_iota(jnp.int32, (RCH, LANES), 0).astype(F32)
    carry = []
    for d in range(2):
        lag = qi - si if d == 0 else si - qi
        decay_scr[d] = jnp.where(lag >= 0, jnp.exp(jnp.maximum(lag, 0.0) * log_gamma(d, RCH)), 0.0)
        lg = log_gamma(d, LANES)
        vec_scr[2 * d] = jnp.exp((ri + 1.0 if d == 0 else RCH - ri) * lg)
        vec_scr[2 * d + 1] = jnp.exp((RCH - 1.0 - ri if d == 0 else ri) * lg)
        carry.append(jnp.exp(RCH * log_gamma(d, C_DV)))

    def step(d, i):
        rows = pl.ds(order[d][i] * RCH, RCH)
        qb, kb, vb = q_ref[rows, :], k_ref[rows, :], v_ref[rows, :]
        att = _scores_t(qb, kb) * decay_scr[d]
        o = jnp.dot(att.astype(BF16), vb, preferred_element_type=F32)
        if i > 0:
            cross = jnp.dot(qb, state_scr[d].astype(BF16), preferred_element_type=F32)
            o = o + _scale_rows(cross, vec_scr[2 * d])
        if i < n - 1:
            kd = _scale_rows(kb.astype(F32), vec_scr[2 * d + 1]).astype(BF16)
            upd = lax.dot_general(kd, vb, (((0,), (0,)), ((), ())), preferred_element_type=F32)
            state_scr[d] = upd if i == 0 else state_scr[d] * carry[d] + upd
        return o

    def finish(c, o):
        rows = pl.ds(c * RCH, RCH)
        gate = g_ref[rows, :].astype(F32)
        y = _rms(o) * gn_ref[...] * (gate * _sigmoid(gate))
        o_ref[rows, :] = y.astype(o_ref.dtype)

    for i in range(n):
        outs = [step(d, i) for d in range(2)]
        if order[0][i] == order[1][i]:
            finish(order[0][i], outs[0] + outs[1])
            continue
        for d in range(2):
            c = order[d][i]
            rows = pl.ds(c * RCH, RCH)
            if when[1 - d][c] < i:
                finish(c, outs[d] + part_scr[rows, :])
            else:
                part_scr[rows, :] = outs[d]


def _retention(logit, u, gn, n_lat):
    nb, t, _ = u.shape
    return pl.pallas_call(
        functools.partial(_retention_kernel, n_lat=n_lat),
        grid=(nb, C_HEADS),
        in_specs=[
            pl.BlockSpec(memory_space=pltpu.SMEM),
            pl.BlockSpec((None, t, C_DK), lambda b, h: (b, 0, U_CQ * UNIT // C_DK + h)),
            pl.BlockSpec((None, t, C_DK), lambda b, h: (b, 0, U_CK * UNIT // C_DK + h)),
            pl.BlockSpec((None, t, C_DV), lambda b, h: (b, 0, U_CV * UNIT // C_DV + h)),
            pl.BlockSpec((None, t, C_DV), lambda b, h: (b, 0, U_CG * UNIT // C_DV + h)),
            pl.BlockSpec((1, C_DV), lambda b, h: (0, h)),
        ],
        out_specs=pl.BlockSpec((None, t, C_DV), lambda b, h: (b, 0, h)),
        out_shape=jax.ShapeDtypeStruct((nb, t, C_HEADS * C_DV), BF16),
        scratch_shapes=[
            pltpu.VMEM((t, C_DV), F32),
            pltpu.VMEM((2, C_DK, C_DV), F32),
            pltpu.VMEM((2, RCH, RCH), F32),
            pltpu.VMEM((4, RCH, LANES), F32),
        ],
        compiler_params=_params("parallel", "parallel"),
        name="retention",
    )(logit, u, u, u, u, gn.reshape(1, C_HEADS * C_DV))


def _merge_kernel(ya_ref, yb_ref, yc_ref, ga_ref, gb_ref, gc_ref, x_ref, modb_ref, modc_ref, g2_ref,
                  wpa_ref, wpb_ref, wpc_ref, wo_ref, x_out_ref, h_out_ref, *, n_lat):
    rows = x_ref.shape[0]
    is_ctx = _ctx_rows(pl.program_id(1), rows, n_lat)

    def branch(y_ref, w_ref, gate_ref):
        p = jnp.dot(y_ref[...], w_ref[...], preferred_element_type=F32)
        return _sigmoid(gate_ref[...].astype(F32)) * p

    y = branch(ya_ref, wpa_ref, ga_ref) + branch(yb_ref, wpb_ref, gb_ref) + branch(yc_ref, wpc_ref, gc_ref)
    o = jnp.dot(y.astype(BF16), wo_ref[...], preferred_element_type=F32)
    x = x_ref[...] + _mod(modb_ref, modc_ref, 2, is_ctx) * o
    x_out_ref[...] = x
    h = _rms(x) * g2_ref[...]
    h = h * (1.0 + _mod(modb_ref, modc_ref, 4, is_ctx)) + _mod(modb_ref, modc_ref, 3, is_ctx)
    h_out_ref[...] = h.astype(h_out_ref.dtype)


def _merge(ya, yb, yc, u, x, mod, g2, wpa, wpb, wpc, wo, n_lat):
    nb, t, d = x.shape
    row = lambda b, j: (b, j, 0)
    full = lambda b, j: (0, 0)
    gate = lambda unit: pl.BlockSpec((None, ROW_BLOCK, d), lambda b, j: (b, j, unit * UNIT // d))
    return pl.pallas_call(
        functools.partial(_merge_kernel, n_lat=n_lat),
        grid=(nb, t // ROW_BLOCK),
        in_specs=[
            pl.BlockSpec((None, ROW_BLOCK, ya.shape[2]), row),
            pl.BlockSpec((None, ROW_BLOCK, yb.shape[2]), row),
            pl.BlockSpec((None, ROW_BLOCK, yc.shape[2]), row),
            gate(U_GA), gate(U_GB), gate(U_GC),
            pl.BlockSpec((None, ROW_BLOCK, d), row),
        ] + _mod_specs(d, nb) + [
            pl.BlockSpec((1, d), full),
            pl.BlockSpec(wpa.shape, full),
            pl.BlockSpec(wpb.shape, full),
            pl.BlockSpec(wpc.shape, full),
            pl.BlockSpec(wo.shape, full),
        ],
        out_specs=[pl.BlockSpec((None, ROW_BLOCK, d), row), pl.BlockSpec((None, ROW_BLOCK, d), row)],
        out_shape=[jax.ShapeDtypeStruct((nb, t, d), F32), jax.ShapeDtypeStruct((nb, t, d), BF16)],
        compiler_params=_params("parallel", "parallel"),
        name="merge",
    )(ya, yb, yc, u, u, u, x, mod, mod, g2.reshape(1, d), wpa, wpb, wpc, wo)


def _mlp_kernel(h_ref, x_ref, modb_ref, modc_ref, nmodb_ref, nmodc_ref, gn_ref, w1_ref, w2_ref,
                *out_refs, n_lat, ff_block, final):
    rows = x_ref.shape[0]
    is_ctx = _ctx_rows(pl.program_id(1), rows, n_lat)
    h = h_ref[...]
    acc = jnp.zeros(x_ref.shape, F32)
    for f in range(w1_ref.shape[1] // ff_block):
        cols = slice(f * ff_block, (f + 1) * ff_block)
        a = jnp.maximum(jnp.dot(h, w1_ref[:, cols], preferred_element_type=F32), 0.0)
        acc = acc + jnp.dot((a * a).astype(BF16), w2_ref[cols, :], preferred_element_type=F32)
    x = x_ref[...] + _mod(modb_ref, modc_ref, 5, is_ctx) * acc
    y = _rms(x) * gn_ref[...]
    if final:
        out_refs[0][...] = y
    else:
        out_refs[0][...] = x
        hn = y * (1.0 + _mod(nmodb_ref, nmodc_ref, 1, is_ctx)) + _mod(nmodb_ref, nmodc_ref, 0, is_ctx)
        out_refs[1][...] = hn.astype(out_refs[1].dtype)


def _mlp(h, x, mod, next_mod, gn, w1, w2, n_lat, final):
    nb, t, d = x.shape
    row = lambda b, j: (b, j, 0)
    full = lambda b, j: (0, 0)
    block = pl.BlockSpec((None, ROW_BLOCK, d), row)
    if final:
        out_specs = [block]
        out_shape = [jax.ShapeDtypeStruct((nb, n_lat, d), F32)]
    else:
        out_specs = [block, block]
        out_shape = [jax.ShapeDtypeStruct((nb, t, d), F32), jax.ShapeDtypeStruct((nb, t, d), BF16)]
    return pl.pallas_call(
        functools.partial(_mlp_kernel, n_lat=n_lat, ff_block=1024, final=final),
        grid=(nb, t // ROW_BLOCK),
        in_specs=[block, block] + _mod_specs(d, nb) + _mod_specs(d, nb) + [
            pl.BlockSpec((1, d), full),
            pl.BlockSpec(w1.shape, full, pipeline_mode=pl.Buffered(1)),
            pl.BlockSpec(w2.shape, full, pipeline_mode=pl.Buffered(1)),
        ],
        out_specs=out_specs,
        out_shape=out_shape,
        compiler_params=_params("parallel", "parallel"),
        name="mlp",
    )(h, x, mod, mod, next_mod, next_mod, gn.reshape(1, d), w1, w2)


def _rope_tables(n_lat, n_ctx):
    rows = n_lat // GRID_W
    row = jnp.arange(rows).astype(F32)
    col = jnp.arange(GRID_W).astype(F32)

    def cos_sin(dim):
        n = dim // 4
        inv = ROPE_THETA ** (-jnp.arange(n, dtype=F32) / n)
        ang_r, ang_c = row[:, None] * inv, col[:, None] * inv
        spread = lambda fn: jnp.concatenate(
            [jnp.repeat(fn(ang_r), GRID_W, axis=0), jnp.tile(fn(ang_c), (rows, 1))], axis=-1)
        return spread(jnp.cos), spread(jnp.sin)

    def with_ctx(tab, fill):
        return jnp.concatenate([tab, jnp.full((n_ctx, tab.shape[1]), fill, F32)], axis=0)

    cos_h, sin_h = cos_sin(A_HD)
    cos_r, sin_r = cos_sin(C_DK)
    return (
        with_ctx(jnp.concatenate([cos_h, cos_h], axis=-1), 1.0),
        with_ctx(jnp.concatenate([-sin_h, sin_h], axis=-1), 0.0),
        with_ctx(cos_r, 1.0),
        with_ctx(sin_r, 0.0),
    )


def kernel(x, c, ctx, c_ctx, w_ada, b_ada, norm1_g, norm2_g, w_in, qn_g, kn_g, sink, ret_logit, ret_norm_g,
           w_pa, w_pb, w_pc, w_o, w_ff1, w_ff2, final_g):
    nb, n_lat, d = x.shape
    n_ctx = ctx.shape[1]
    depth = w_ada.shape[0]
    t = n_lat + n_ctx
    assert t % ROW_BLOCK == 0 and n_lat % TQ == 0 and n_ctx % TQ == 0 and n_lat % GRID_W == 0
    assert n_lat >= TQ + 2 * WINDOW and w_in.shape[2] == N_UNITS * UNIT

    tabs = _rope_tables(n_lat, n_ctx)
    mod_rows = 16
    craw = jnp.concatenate([c, c_ctx[None], jnp.zeros((mod_rows - nb - 1, d), F32)], axis=0)
    mod = _ada(craw, w_ada, b_ada).reshape(depth, mod_rows, N_MOD, d)

    xs, h = _prenorm(x, ctx, mod[0], norm1_g[0])
    for l in range(depth):
        u = _inproj(h, w_in, l, tabs, qn_g[l], kn_g[l])
        ya = _attn_a(u, n_lat)
        yb = _attn_b(sink[l], u, n_lat)
        yc = _retention(ret_logit[l], u, ret_norm_g[l], n_lat)
        xs, h2 = _merge(ya, yb, yc, u, xs, mod[l], norm2_g[l], w_pa[l].astype(BF16), w_pb[l].astype(BF16),
                        w_pc[l].astype(BF16), w_o[l].astype(BF16), n_lat)
        final = l == depth - 1
        gn = final_g if final else norm1_g[l + 1]
        outs = _mlp(h2, xs, mod[l], mod[l if final else l + 1], gn, w_ff1[l].astype(BF16),
                    w_ff2[l].astype(BF16), n_lat, final)
        if final:
            return outs[0]
        xs, h = outs
```

```python
import functools
import math

import jax
import jax.numpy as jnp
from jax import lax
from jax.experimental import pallas as pl
from jax.experimental.pallas import tpu as pltpu

F32 = jnp.float32
BF16 = jnp.bfloat16

GRID_W = 64
WINDOW = 128
ROPE_THETA = 10000.0
EPS = 1e-6
NEG = -1e30
A_HEADS, A_KV, A_HD = 8, 2, 128
B_HEADS, B_KV, B_HD = 8, 2, 128
C_HEADS, C_DK, C_DV = 4, 256, 512
N_MOD = 6
LOG2E = math.log2(math.e)

LANES = 128
MXU_DIM = 256
V7X_VMEM_BYTES = 64 * 1024 * 1024
VMEM_LIMIT = V7X_VMEM_BYTES - 8 * 1024 * 1024

ROW_BLOCK = 768
TQ = MXU_DIM
KCH = MXU_DIM
RCH = MXU_DIM
UNIT = LANES

U_AQ, U_BQ = 0, 8
U_AK, U_BK, U_AV, U_BV = 16, 18, 20, 22
U_CQ, U_CK, U_CV, U_CG = 24, 32, 40, 56
U_GA, U_GB, U_GC = 72, 80, 88
N_UNITS = 96
_UNIT_PERM = (list(range(0, 8)) + list(range(12, 20)) + [8, 9, 20, 21, 10, 11, 22, 23]
              + list(range(24, 96)))
IN_BLOCK_UNITS = 8
IN_PAIR = MXU_DIM // UNIT


def _params(*sem):
    return pltpu.CompilerParams(dimension_semantics=sem, vmem_limit_bytes=VMEM_LIMIT)


def _rms(x):
    return x * lax.rsqrt(jnp.mean(x * x, axis=-1, keepdims=True) + EPS)


def _sigmoid(x):
    return 1.0 / (1.0 + jnp.exp2(x * -LOG2E))


def _ctx_rows(j, rows, n_lat):
    r = j * rows + lax.broadcasted_iota(jnp.int32, (rows, 1), 0)
    return r >= n_lat


def _mod(modb_ref, modc_ref, idx, is_ctx):
    return jnp.where(is_ctx, modc_ref[idx:idx + 1, :], modb_ref[idx:idx + 1, :])


def _ada_kernel(c_ref, w_ref, b_ref, o_ref):
    c = c_ref[...]
    s = (c * _sigmoid(c)).astype(BF16)
    o_ref[...] = jnp.dot(s, w_ref[...].astype(BF16), preferred_element_type=F32) + b_ref[...]


def _ada(craw, w_ada, b_ada):
    depth, d, width = w_ada.shape
    rows = craw.shape[0]
    tn = 1536
    return pl.pallas_call(
        _ada_kernel,
        grid=(depth, width // tn),
        in_specs=[
            pl.BlockSpec((rows, d), lambda l, j: (0, 0)),
            pl.BlockSpec((None, d, tn), lambda l, j: (l, 0, j)),
            pl.BlockSpec((None, 1, tn), lambda l, j: (l, 0, j)),
        ],
        out_specs=pl.BlockSpec((None, rows, tn), lambda l, j: (l, 0, j)),
        out_shape=jax.ShapeDtypeStruct((depth, rows, width), F32),
        compiler_params=_params("parallel", "parallel"),
        name="ada",
    )(craw, w_ada, b_ada.reshape(depth, 1, width))


def _prenorm_kernel(x_ref, ctx_ref, modb_ref, modc_ref, g_ref, xs_ref, h_ref, *, n_lat):
    rows = x_ref.shape[0]
    j = pl.program_id(1)
    is_ctx = _ctx_rows(j, rows, n_lat)
    x = jnp.where(j * rows < n_lat, x_ref[...], ctx_ref[...])
    xs_ref[...] = x
    y = _rms(x) * g_ref[...]
    h = y * (1.0 + _mod(modb_ref, modc_ref, 1, is_ctx)) + _mod(modb_ref, modc_ref, 0, is_ctx)
    h_ref[...] = h.astype(h_ref.dtype)


def _mod_specs(d, n_batch):
    return [
        pl.BlockSpec((None, N_MOD, d), lambda b, j: (b, 0, 0)),
        pl.BlockSpec((None, N_MOD, d), lambda b, j: (n_batch, 0, 0)),
    ]


def _prenorm(x, ctx, mod, g):
    nb, n_lat, d = x.shape
    n_ctx = ctx.shape[1]
    rows = math.gcd(n_lat, n_ctx)
    n_lat_blocks = n_lat // rows
    row = lambda b, j: (b, j, 0)
    block = pl.BlockSpec((None, rows, d), row)
    return pl.pallas_call(
        functools.partial(_prenorm_kernel, n_lat=n_lat),
        grid=(nb, (n_lat + n_ctx) // rows),
        in_specs=[
            pl.BlockSpec((None, rows, d), lambda b, j: (b, jnp.minimum(j, n_lat_blocks - 1), 0)),
            pl.BlockSpec((None, rows, d), lambda b, j: (b, jnp.maximum(j - n_lat_blocks, 0), 0)),
        ] + _mod_specs(d, nb) + [pl.BlockSpec((1, d), lambda b, j: (0, 0))],
        out_specs=[block, block],
        out_shape=[jax.ShapeDtypeStruct((nb, n_lat + n_ctx, d), F32),
                   jax.ShapeDtypeStruct((nb, n_lat + n_ctx, d), BF16)],
        compiler_params=_params("parallel", "parallel"),
        name="prenorm",
    )(x, ctx, mod, mod, g.reshape(1, d))


def _rope128(a, c, s):
    return a * c + pltpu.roll(a, LANES // 2, axis=1) * s


def _rms_heads(a):
    r = lax.broadcasted_iota(jnp.int32, (MXU_DIM, MXU_DIM), 0) // LANES
    c = lax.broadcasted_iota(jnp.int32, (MXU_DIM, MXU_DIM), 1) // LANES
    ones = (r == c).astype(BF16)
    sq = a * a
    hi = sq.astype(BF16)
    lo = (sq - hi.astype(F32)).astype(BF16)
    tot = (jnp.dot(hi, ones, preferred_element_type=F32)
           + jnp.dot(lo, ones, preferred_element_type=F32))
    return a * lax.rsqrt(tot * (1.0 / LANES) + EPS)


def _inproj_kernel(src_ref, h_ref, *refs):
    del src_ref
    n_pairs = IN_BLOCK_UNITS // IN_PAIR
    w_refs = refs[:n_pairs]
    c128_ref, s128_ref, c256_ref, s256_ref, qn_ref, kn_ref, o_ref, w_ref = refs[n_pairs:]
    j = pl.program_id(0)
    t = h_ref.shape[0]

    @pl.when(pl.program_id(1) == 0)
    def _():
        for p, wp_ref in enumerate(w_refs):
            w_ref[:, p * MXU_DIM:(p + 1) * MXU_DIM] = wp_ref[...].astype(w_ref.dtype)

    def for_pairs(fn):
        steps = [(pl.ds(ci * ROW_BLOCK, ROW_BLOCK), u)
                 for ci in range(t // ROW_BLOCK) for u in range(0, IN_BLOCK_UNITS, IN_PAIR)]
        pending = None
        for rows, u in steps + [(None, None)]:
            if rows is not None:
                cols = slice(u * UNIT, (u + IN_PAIR) * UNIT)
                a = jnp.dot(h_ref[rows, :], w_ref[:, cols], preferred_element_type=F32)
            if pending is not None:
                p_rows, p_u, p_a = pending
                o_ref[p_rows, p_u * UNIT:(p_u + IN_PAIR) * UNIT] = fn(p_u, p_a, p_rows).astype(o_ref.dtype)
            pending = (rows, u, a) if rows is not None else None

    def rope_heads(a, rows, gain=None, scale=None):
        out = []
        for a1 in (a[:, :UNIT], a[:, UNIT:]):
            if gain is not None:
                a1 = a1 * gain
            a1 = _rope128(a1, c128_ref[rows, :], s128_ref[rows, :])
            out.append(a1 if scale is None else a1 * scale)
        return jnp.concatenate(out, axis=1)

    @pl.when(j == U_AQ // IN_BLOCK_UNITS)
    def _():
        scale = A_HD ** -0.5 * LOG2E
        for_pairs(lambda u, a, rows: rope_heads(_rms_heads(a), rows, qn_ref[...], scale))

    @pl.when(j == U_BQ // IN_BLOCK_UNITS)
    def _():
        scale = B_HD ** -0.5 * LOG2E
        for_pairs(lambda u, a, rows: rope_heads(a, rows, scale=scale))

    @pl.when(j == U_AK // IN_BLOCK_UNITS)
    def _():
        def fn(u, a, rows):
            if u < U_BK - U_AK:
                return rope_heads(_rms_heads(a), rows, kn_ref[...])
            if u < U_AV - U_AK:
                return rope_heads(a, rows)
            return a
        for_pairs(fn)

    def rope_r(scale):
        def fn(u, a, rows):
            a0, a1 = a[:, :UNIT], a[:, UNIT:]
            c, s = c256_ref[rows, :], s256_ref[rows, :]
            return jnp.concatenate([(a0 * c - a1 * s) * scale, (a0 * s + a1 * c) * scale], axis=1)
        return fn

    @pl.when(j == U_CQ // IN_BLOCK_UNITS)
    def _():
        for_pairs(rope_r(1.0))

    @pl.when(j == U_CK // IN_BLOCK_UNITS)
    def _():
        for_pairs(rope_r(C_DK ** -0.5))

    @pl.when(j >= U_CV // IN_BLOCK_UNITS)
    def _():
        for_pairs(lambda u, a, rows: a)


def _inproj(h, w_in, layer, tabs, qn, kn):
    nb, t, d = h.shape
    width = w_in.shape[2]
    tn = IN_BLOCK_UNITS * UNIT
    n_pairs = IN_BLOCK_UNITS // IN_PAIR
    assert all(a % IN_PAIR == 0 and b == a + 1 for a, b in zip(_UNIT_PERM[0::2], _UNIT_PERM[1::2]))
    src = jnp.asarray([a // IN_PAIR for a in _UNIT_PERM[0::IN_PAIR]], jnp.int32)
    tab = pl.BlockSpec((t, LANES), lambda j, b, src: (0, 0))
    gain = pl.BlockSpec((1, LANES), lambda j, b, src: (0, 0))

    def pair(p):
        return pl.BlockSpec((None, d, MXU_DIM), lambda j, b, src: (layer, 0, src[j * n_pairs + p]))

    return pl.pallas_call(
        _inproj_kernel,
        grid_spec=pltpu.PrefetchScalarGridSpec(
            num_scalar_prefetch=1,
            grid=(width // tn, nb),
            in_specs=[pl.BlockSpec((None, t, d), lambda j, b, src: (b, 0, 0))]
            + [pair(p) for p in range(n_pairs)] + [tab, tab, tab, tab, gain, gain],
            out_specs=pl.BlockSpec((None, t, tn), lambda j, b, src: (b, 0, j)),
            scratch_shapes=[pltpu.VMEM((d, tn), BF16)],
        ),
        out_shape=jax.ShapeDtypeStruct((nb, t, width), BF16),
        compiler_params=_params("arbitrary", "arbitrary"),
        name="inproj",
    )(src, h, *([w_in] * n_pairs), *tabs, qn.reshape(1, LANES), kn.reshape(1, LANES))


def _scores_t(k, q):
    return lax.dot_general(k, q, (((1,), (1,)), ((), ())), preferred_element_type=F32)


def _stack_heads(q_ref, group, hd):
    return jnp.concatenate([q_ref[:, g * hd:(g + 1) * hd] for g in range(group)], axis=0)


def _store_heads(o_ref, o, group, hd):
    for g in range(group):
        o_ref[:, g * hd:(g + 1) * hd] = o[g * TQ:(g + 1) * TQ, :].astype(o_ref.dtype)


def _softmax_pipeline(i, n_q_lat, q_ref, o_ref, s_scr, m_scr, lat_chunks, ctx_chunks,
                      keys, values_t, bias, sink, group, hd):
    def rows(c):
        return pl.ds(c * KCH, KCH)

    def score_chunk(q, c, m):
        s = _scores_t(keys(c, i), q)
        b = bias(c, i)
        if b is not None:
            s = jnp.concatenate([s[:, g * TQ:(g + 1) * TQ] + b for g in range(group)], axis=1)
        s_scr[rows(c), :] = s
        mc = jnp.max(s, axis=0, keepdims=True)
        return mc if m is None else jnp.maximum(m, mc)

    def value_chunk(c, m, den, o):
        p = jnp.exp2(s_scr[rows(c), :] - m)
        dc = jnp.sum(p, axis=0, keepdims=True)
        oc = jnp.dot(values_t(c, i - 1), p.astype(BF16), preferred_element_type=F32)
        return (dc if den is None else den + dc), (oc if o is None else o + oc)

    def run(score_chunks, value_chunks):
        q = _stack_heads(q_ref, group, hd) if score_chunks else None
        m_old = m_scr[0:1, :] if value_chunks else None
        m_new = sink
        den = None if sink is None or not value_chunks else jnp.exp2(sink - m_old)
        o = None
        for c in sorted(set(score_chunks) | set(value_chunks)):
            if c in value_chunks:
                den, o = value_chunk(c, m_old, den, o)
            if c in score_chunks:
                m_new = score_chunk(q, c, m_new)
        if value_chunks:
            _store_heads(o_ref, (o * (1.0 / den)).T, group, hd)
        if score_chunks:
            m_scr[0:1, :] = m_new

    @pl.when(i == 0)
    def _():
        run(lat_chunks, ())

    @pl.when((i > 0) & (i < n_q_lat))
    def _():
        run(lat_chunks, lat_chunks)

    @pl.when(i == n_q_lat)
    def _():
        run(ctx_chunks, lat_chunks)

    @pl.when(i > n_q_lat)
    def _():
        run((), ctx_chunks)


def _attn_a_kernel(q_ref, k_ref, v_ref, o_ref, vt_scr, s_scr, m_scr, *, n_lat, group, hd):
    i = pl.program_id(2)
    t = k_ref.shape[0]

    @pl.when(i == 0)
    def _():
        vt_scr[...] = v_ref[...].T

    _softmax_pipeline(
        i, n_lat // TQ, q_ref, o_ref, s_scr, m_scr,
        lat_chunks=range(t // KCH), ctx_chunks=range(n_lat // KCH, t // KCH),
        keys=lambda c, blk: k_ref[pl.ds(c * KCH, KCH), :],
        values_t=lambda c, blk: vt_scr[:, pl.ds(c * KCH, KCH)],
        bias=lambda c, blk: None, sink=None, group=group, hd=hd)


def _attn_a(u, n_lat):
    nb, t, _ = u.shape
    group = A_HEADS // A_KV
    qw = group * A_HD
    n_q = t // TQ
    assert n_q == n_lat // TQ + 1, "one context query block expected"
    return pl.pallas_call(
        functools.partial(_attn_a_kernel, n_lat=n_lat, group=group, hd=A_HD),
        grid=(nb, A_KV, n_q + 1),
        in_specs=[
            pl.BlockSpec((None, TQ, qw),
                         lambda b, kv, i: (b, jnp.minimum(i, n_q - 1), U_AQ * UNIT // qw + kv)),
            pl.BlockSpec((None, t, A_HD), lambda b, kv, i: (b, 0, U_AK + kv)),
            pl.BlockSpec((None, t, A_HD), lambda b, kv, i: (b, 0, U_AV + kv)),
        ],
        out_specs=pl.BlockSpec((None, TQ, qw), lambda b, kv, i: (b, jnp.maximum(i - 1, 0), kv)),
        out_shape=jax.ShapeDtypeStruct((nb, t, A_HEADS * A_HD), BF16),
        scratch_shapes=[
            pltpu.VMEM((A_HD, t), BF16),
            pltpu.VMEM((t, group * TQ), F32),
            pltpu.VMEM((8, group * TQ), F32),
        ],
        compiler_params=_params("parallel", "parallel", "arbitrary"),
        name="attn_a",
    )(u, u, u)


def _attn_b_kernel(sink_ref, q_ref, k_ref, v_ref, o_ref, band_scr, s_scr, m_scr, *, n_lat, group, hd):
    kv = pl.program_id(1)
    i = pl.program_id(2)
    span = TQ + 2 * WINDOW
    sink = jnp.concatenate(
        [jnp.full((1, TQ), sink_ref[kv * group + g], F32) for g in range(group)], axis=1) * LOG2E

    @pl.when(i == 0)
    def _():
        r = lax.broadcasted_iota(jnp.int32, (span, TQ), 0)
        c = lax.broadcasted_iota(jnp.int32, (span, TQ), 1)
        for lead in range(band_scr.shape[0]):
            band_scr[lead] = jnp.where(jnp.abs(c + lead * WINDOW - r) <= WINDOW, 0.0, NEG)

    def start(blk):
        return jnp.clip(blk * TQ - WINDOW, 0, n_lat - span)

    def key_rows(c, blk):
        if c == 0:
            return pl.ds(n_lat, KCH)
        return pl.ds(pl.multiple_of(start(blk) + (c - 1) * KCH, WINDOW), KCH)

    def band(c, blk):
        if c == 0:
            return None
        return band_scr[(blk * TQ - start(blk)) // WINDOW, pl.ds((c - 1) * KCH, KCH), :]

    _softmax_pipeline(
        i, n_lat // TQ, q_ref, o_ref, s_scr, m_scr,
        lat_chunks=range(1 + span // KCH), ctx_chunks=range(1),
        keys=lambda c, blk: k_ref[key_rows(c, blk), :],
        values_t=lambda c, blk: v_ref[key_rows(c, blk), :].T,
        bias=band, sink=sink, group=group, hd=hd)


def _attn_b(sink, u, n_lat):
    nb, t, _ = u.shape
    group = B_HEADS // B_KV
    qw = group * B_HD
    n_q = t // TQ
    span = TQ + 2 * WINDOW
    assert n_q == n_lat // TQ + 1 and t - n_lat == KCH and span % KCH == 0
    return pl.pallas_call(
        functools.partial(_attn_b_kernel, n_lat=n_lat, group=group, hd=B_HD),
        grid=(nb, B_KV, n_q + 1),
        in_specs=[
            pl.BlockSpec(memory_space=pltpu.SMEM),
            pl.BlockSpec((None, TQ, qw),
                         lambda b, kv, i: (b, jnp.minimum(i, n_q - 1), U_BQ * UNIT // qw + kv)),
            pl.BlockSpec((None, t, B_HD), lambda b, kv, i: (b, 0, U_BK + kv)),
            pl.BlockSpec((None, t, B_HD), lambda b, kv, i: (b, 0, U_BV + kv)),
        ],
        out_specs=pl.BlockSpec((None, TQ, qw), lambda b, kv, i: (b, jnp.maximum(i - 1, 0), kv)),
        out_shape=jax.ShapeDtypeStruct((nb, t, B_HEADS * B_HD), BF16),
        scratch_shapes=[
            pltpu.VMEM((3, span, TQ), F32),
            pltpu.VMEM((KCH + span, group * TQ), F32),
            pltpu.VMEM((8, group * TQ), F32),
        ],
        compiler_params=_params("parallel", "parallel", "arbitrary"),
        name="attn_b",
    )(sink, u, u, u)


def _scale_rows(x, v):
    return jnp.concatenate(
        [x[:, u * LANES:(u + 1) * LANES] * v for u in range(x.shape[1] // LANES)], axis=1)


def _retention_kernel(logit_ref, q_ref, k_ref, v_ref, g_ref, gn_ref, o_ref,
                      part_scr, state_scr, decay_scr, vec_scr, *, n_lat):
    h = pl.program_id(1)
    t = q_ref.shape[0]
    n_l, n_c = n_lat // RCH, (t - n_lat) // RCH
    n = n_l + n_c
    order = ([n_l + c for c in range(n_c)] + list(range(n_l)),
             [n_l + c for c in reversed(range(n_c))] + list(reversed(range(n_l))))
    when = tuple({c: i for i, c in enumerate(o)} for o in order)

    def log_gamma(direction, width):
        x = jnp.full((1, width), logit_ref[direction, h], F32)
        return jnp.minimum(x, 0.0) - jnp.log1p(jnp.exp(-jnp.abs(x)))

    qi = lax.broadcasted_iota(jnp.int32, (RCH, RCH), 0).astype(F32)
    si = lax.broadcasted_iota(jnp.int32, (RCH, RCH), 1).astype(F32)
    ri = lax.broadcasted_iota(jnp.int32, (RCH, LANES), 0).astype(F32)
    carry = []
    for d in range(2):
        lag = qi - si if d == 0 else si - qi
        decay_scr[d] = jnp.where(lag >= 0, jnp.exp(jnp.maximum(lag, 0.0) * log_gamma(d, RCH)), 0.0)
        lg = log_gamma(d, LANES)
        vec_scr[2 * d] = jnp.exp((ri + 1.0 if d == 0 else RCH - ri) * lg)
        vec_scr[2 * d + 1] = jnp.exp((RCH - 1.0 - ri if d == 0 else ri) * lg)
        carry.append(jnp.exp(RCH * log_gamma(d, C_DV)))

    def step(d, i):
        rows = pl.ds(order[d][i] * RCH, RCH)
        qb, kb, vb = q_ref[rows, :], k_ref[rows, :], v_ref[rows, :]
        att = _scores_t(qb, kb) * decay_scr[d]
        o = jnp.dot(att.astype(BF16), vb, preferred_element_type=F32)
        if i > 0:
            qd = _scale_rows(qb.astype(F32), vec_scr[2 * d]).astype(BF16)
            o = o + jnp.dot(qd, state_scr[d].astype(BF16), preferred_element_type=F32)
        if i < n - 1:
            kd = _scale_rows(kb.astype(F32), vec_scr[2 * d + 1]).astype(BF16)
            upd = lax.dot_general(kd, vb, (((0,), (0,)), ((), ())), preferred_element_type=F32)
            state_scr[d] = upd if i == 0 else state_scr[d] * carry[d] + upd
        return o

    def finish(c, o):
        rows = pl.ds(c * RCH, RCH)
        gate = g_ref[rows, :].astype(F32)
        y = _rms(o) * gn_ref[...] * (gate * _sigmoid(gate))
        o_ref[rows, :] = y.astype(o_ref.dtype)

    for i in range(n):
        outs = [step(d, i) for d in range(2)]
        if order[0][i] == order[1][i]:
            finish(order[0][i], outs[0] + outs[1])
            continue
        for d in range(2):
            c = order[d][i]
            rows = pl.ds(c * RCH, RCH)
            if when[1 - d][c] < i:
                finish(c, outs[d] + part_scr[rows, :])
            else:
                part_scr[rows, :] = outs[d]


def _retention(logit, u, gn, n_lat):
    nb, t, _ = u.shape
    return pl.pallas_call(
        functools.partial(_retention_kernel, n_lat=n_lat),
        grid=(nb, C_HEADS),
        in_specs=[
            pl.BlockSpec(memory_space=pltpu.SMEM),
            pl.BlockSpec((None, t, C_DK), lambda b, h: (b, 0, U_CQ * UNIT // C_DK + h)),
            pl.BlockSpec((None, t, C_DK), lambda b, h: (b, 0, U_CK * UNIT // C_DK + h)),
            pl.BlockSpec((None, t, C_DV), lambda b, h: (b, 0, U_CV * UNIT // C_DV + h)),
            pl.BlockSpec((None, t, C_DV), lambda b, h: (b, 0, U_CG * UNIT // C_DV + h)),
            pl.BlockSpec((1, C_DV), lambda b, h: (0, h)),
        ],
        out_specs=pl.BlockSpec((None, t, C_DV), lambda b, h: (b, 0, h)),
        out_shape=jax.ShapeDtypeStruct((nb, t, C_HEADS * C_DV), BF16),
        scratch_shapes=[
            pltpu.VMEM((t, C_DV), F32),
            pltpu.VMEM((2, C_DK, C_DV), F32),
            pltpu.VMEM((2, RCH, RCH), F32),
            pltpu.VMEM((4, RCH, LANES), F32),
        ],
        compiler_params=_params("parallel", "parallel"),
        name="retention",
    )(logit, u, u, u, u, gn.reshape(1, C_HEADS * C_DV))


def _merge_kernel(ya_ref, yb_ref, yc_ref, ga_ref, gb_ref, gc_ref, x_ref, modb_ref, modc_ref, g2_ref,
                  wpa_ref, wpb_ref, wpc_ref, wo_ref, x_out_ref, h_out_ref, *, n_lat):
    rows = x_ref.shape[0]
    is_ctx = _ctx_rows(pl.program_id(1), rows, n_lat)

    def branch(y_ref, w_ref, gate_ref):
        p = jnp.dot(y_ref[...], w_ref[...], preferred_element_type=F32)
        return _sigmoid(gate_ref[...].astype(F32)) * p

    y = branch(ya_ref, wpa_ref, ga_ref) + branch(yb_ref, wpb_ref, gb_ref) + branch(yc_ref, wpc_ref, gc_ref)
    o = jnp.dot(y.astype(BF16), wo_ref[...], preferred_element_type=F32)
    x = x_ref[...] + _mod(modb_ref, modc_ref, 2, is_ctx) * o
    x_out_ref[...] = x
    h = _rms(x) * g2_ref[...]
    h = h * (1.0 + _mod(modb_ref, modc_ref, 4, is_ctx)) + _mod(modb_ref, modc_ref, 3, is_ctx)
    h_out_ref[...] = h.astype(h_out_ref.dtype)


def _layer_weight(w, layer, **kw):
    return pl.BlockSpec((None,) + w.shape[1:], lambda b, j: (layer, 0, 0), **kw)


def _merge(ya, yb, yc, u, x, mod, g2, wpa, wpb, wpc, wo, layer, n_lat):
    nb, t, d = x.shape
    row = lambda b, j: (b, j, 0)
    full = lambda b, j: (0, 0)
    gate = lambda unit: pl.BlockSpec((None, ROW_BLOCK, d), lambda b, j: (b, j, unit * UNIT // d))
    return pl.pallas_call(
        functools.partial(_merge_kernel, n_lat=n_lat),
        grid=(nb, t // ROW_BLOCK),
        in_specs=[
            pl.BlockSpec((None, ROW_BLOCK, ya.shape[2]), row),
            pl.BlockSpec((None, ROW_BLOCK, yb.shape[2]), row),
            pl.BlockSpec((None, ROW_BLOCK, yc.shape[2]), row),
            gate(U_GA), gate(U_GB), gate(U_GC),
            pl.BlockSpec((None, ROW_BLOCK, d), row),
        ] + _mod_specs(d, nb) + [
            pl.BlockSpec((1, d), full),
            _layer_weight(wpa, layer), _layer_weight(wpb, layer),
            _layer_weight(wpc, layer), _layer_weight(wo, layer),
        ],
        out_specs=[pl.BlockSpec((None, ROW_BLOCK, d), row), pl.BlockSpec((None, ROW_BLOCK, d), row)],
        out_shape=[jax.ShapeDtypeStruct((nb, t, d), F32), jax.ShapeDtypeStruct((nb, t, d), BF16)],
        compiler_params=_params("parallel", "parallel"),
        name="merge",
    )(ya, yb, yc, u, u, u, x, mod, mod, g2.reshape(1, d), wpa, wpb, wpc, wo)


def _mlp_kernel(h_ref, x_ref, modb_ref, modc_ref, nmodb_ref, nmodc_ref, gn_ref, w1_ref, w2_ref,
                *out_refs, n_lat, ff_block, final):
    rows = x_ref.shape[0]
    is_ctx = _ctx_rows(pl.program_id(1), rows, n_lat)
    h = h_ref[...]
    acc = jnp.zeros(x_ref.shape, F32)
    for f in range(w1_ref.shape[1] // ff_block):
        cols = slice(f * ff_block, (f + 1) * ff_block)
        a = jnp.maximum(jnp.dot(h, w1_ref[:, cols], preferred_element_type=F32), 0.0)
        acc = acc + jnp.dot((a * a).astype(BF16), w2_ref[cols, :], preferred_element_type=F32)
    x = x_ref[...] + _mod(modb_ref, modc_ref, 5, is_ctx) * acc
    y = _rms(x) * gn_ref[...]
    if final:
        out_refs[0][...] = y
    else:
        out_refs[0][...] = x
        hn = y * (1.0 + _mod(nmodb_ref, nmodc_ref, 1, is_ctx)) + _mod(nmodb_ref, nmodc_ref, 0, is_ctx)
        out_refs[1][...] = hn.astype(out_refs[1].dtype)


def _mlp(h, x, mod, next_mod, gn, w1, w2, layer, n_lat, final):
    nb, t, d = x.shape
    row = lambda b, j: (b, j, 0)
    full = lambda b, j: (0, 0)
    block = pl.BlockSpec((None, ROW_BLOCK, d), row)
    if final:
        out_specs = [block]
        out_shape = [jax.ShapeDtypeStruct((nb, n_lat, d), F32)]
    else:
        out_specs = [block, block]
        out_shape = [jax.ShapeDtypeStruct((nb, t, d), F32), jax.ShapeDtypeStruct((nb, t, d), BF16)]
    return pl.pallas_call(
        functools.partial(_mlp_kernel, n_lat=n_lat, ff_block=1024, final=final),
        grid=(nb, t // ROW_BLOCK),
        in_specs=[block, block] + _mod_specs(d, nb) + _mod_specs(d, nb) + [
            pl.BlockSpec((1, d), full),
            _layer_weight(w1, layer, pipeline_mode=pl.Buffered(1)),
            _layer_weight(w2, layer, pipeline_mode=pl.Buffered(1)),
        ],
        out_specs=out_specs,
        out_shape=out_shape,
        compiler_params=_params("parallel", "parallel"),
        name="mlp",
    )(h, x, mod, mod, next_mod, next_mod, gn.reshape(1, d), w1, w2)


def _rope_tables(n_lat, n_ctx):
    rows = n_lat // GRID_W
    row = jnp.arange(rows).astype(F32)
    col = jnp.arange(GRID_W).astype(F32)

    def cos_sin(dim):
        n = dim // 4
        inv = ROPE_THETA ** (-jnp.arange(n, dtype=F32) / n)
        ang_r, ang_c = row[:, None] * inv, col[:, None] * inv
        spread = lambda fn: jnp.concatenate(
            [jnp.repeat(fn(ang_r), GRID_W, axis=0), jnp.tile(fn(ang_c), (rows, 1))], axis=-1)
        return spread(jnp.cos), spread(jnp.sin)

    def with_ctx(tab, fill):
        return jnp.concatenate([tab, jnp.full((n_ctx, tab.shape[1]), fill, F32)], axis=0)

    cos_h, sin_h = cos_sin(A_HD)
    cos_r, sin_r = cos_sin(C_DK)
    return (
        with_ctx(jnp.concatenate([cos_h, cos_h], axis=-1), 1.0),
        with_ctx(jnp.concatenate([-sin_h, sin_h], axis=-1), 0.0),
        with_ctx(cos_r, 1.0),
        with_ctx(sin_r, 0.0),
    )


def kernel(x, c, ctx, c_ctx, w_ada, b_ada, norm1_g, norm2_g, w_in, qn_g, kn_g, sink, ret_logit, ret_norm_g,
           w_pa, w_pb, w_pc, w_o, w_ff1, w_ff2, final_g):
    nb, n_lat, d = x.shape
    n_ctx = ctx.shape[1]
    depth = w_ada.shape[0]
    t = n_lat + n_ctx
    assert t % ROW_BLOCK == 0 and n_lat % TQ == 0 and n_ctx % TQ == 0 and n_lat % GRID_W == 0
    assert n_lat >= TQ + 2 * WINDOW and w_in.shape[2] == N_UNITS * UNIT

    tabs = _rope_tables(n_lat, n_ctx)
    mod_rows = 16
    craw = jnp.concatenate([c, c_ctx[None], jnp.zeros((mod_rows - nb - 1, d), F32)], axis=0)
    mod = _ada(craw, w_ada, b_ada).reshape(depth, mod_rows, N_MOD, d)

    merge_w = [w.astype(BF16) for w in (w_pa, w_pb, w_pc, w_o)]
    mlp_w = [w.astype(BF16) for w in (w_ff1, w_ff2)]

    xs, h = _prenorm(x, ctx, mod[0], norm1_g[0])
    for l in range(depth):
        u = _inproj(h, w_in, l, tabs, qn_g[l], kn_g[l])
        ya = _attn_a(u, n_lat)
        yb = _attn_b(sink[l], u, n_lat)
        yc = _retention(ret_logit[l], u, ret_norm_g[l], n_lat)
        xs, h2 = _merge(ya, yb, yc, u, xs, mod[l], norm2_g[l], *merge_w, l, n_lat)
        final = l == depth - 1
        gn = final_g if final else norm1_g[l + 1]
        outs = _mlp(h2, xs, mod[l], mod[l if final else l + 1], gn, *mlp_w, l, n_lat, final)
        if final:
            return outs[0]
        xs, h = outs
```

```python
import functools
import math
from typing import Any, Callable, NamedTuple

import jax
import jax.numpy as jnp
from jax import lax
from jax.experimental import pallas as pl
from jax.experimental.pallas import tpu as pltpu

F32 = jnp.float32
BF16 = jnp.bfloat16

GRID_W = 64
WINDOW = 128
ROPE_THETA = 10000.0
EPS = 1e-6
NEG = -1e30
A_HEADS, A_KV, A_HD = 8, 2, 128
B_HEADS, B_KV, B_HD = 8, 2, 128
C_HEADS, C_DK, C_DV = 4, 256, 512
N_MOD = 6
LOG2E = math.log2(math.e)

LANES = 128
MXU_DIM = 256
V7X_VMEM_BYTES = 64 * 1024 * 1024
VMEM_LIMIT = V7X_VMEM_BYTES - 8 * 1024 * 1024

ROW_BLOCK = 768
TQ = MXU_DIM
KCH = MXU_DIM
RCH = MXU_DIM
UNIT = LANES

U_AQ, U_BQ = 0, 8
U_AK, U_BK, U_AV, U_BV = 16, 18, 20, 22
U_CQ, U_CK, U_CV, U_CG = 24, 32, 40, 56
U_GA, U_GB, U_GC = 72, 80, 88
N_UNITS = 96
_UNIT_PERM = (list(range(0, 8)) + list(range(12, 20)) + [8, 9, 20, 21, 10, 11, 22, 23]
              + list(range(24, 96)))
IN_BLOCK_UNITS = 8
IN_PAIR = MXU_DIM // UNIT


def _params(*sem):
    return pltpu.CompilerParams(dimension_semantics=sem, vmem_limit_bytes=VMEM_LIMIT)


def _rms(x):
    return x * lax.rsqrt(jnp.mean(x * x, axis=-1, keepdims=True) + EPS)


def _sigmoid(x):
    return 1.0 / (1.0 + jnp.exp2(x * -LOG2E))


def _ctx_rows(j, rows, n_lat):
    r = j * rows + lax.broadcasted_iota(jnp.int32, (rows, 1), 0)
    return r >= n_lat


def _mod(modb_ref, modc_ref, idx, is_ctx):
    return jnp.where(is_ctx, modc_ref[idx:idx + 1, :], modb_ref[idx:idx + 1, :])


def _ada_kernel(c_ref, w_ref, b_ref, o_ref):
    c = c_ref[...]
    s = (c * _sigmoid(c)).astype(BF16)
    o_ref[...] = jnp.dot(s, w_ref[...].astype(BF16), preferred_element_type=F32) + b_ref[...]


def _ada(craw, w_ada, b_ada):
    depth, d, width = w_ada.shape
    rows = craw.shape[0]
    tn = 1536
    return pl.pallas_call(
        _ada_kernel,
        grid=(depth, width // tn),
        in_specs=[
            pl.BlockSpec((rows, d), lambda l, j: (0, 0)),
            pl.BlockSpec((None, d, tn), lambda l, j: (l, 0, j)),
            pl.BlockSpec((None, 1, tn), lambda l, j: (l, 0, j)),
        ],
        out_specs=pl.BlockSpec((None, rows, tn), lambda l, j: (l, 0, j)),
        out_shape=jax.ShapeDtypeStruct((depth, rows, width), F32),
        compiler_params=_params("parallel", "parallel"),
        name="ada",
    )(craw, w_ada, b_ada.reshape(depth, 1, width))


def _prenorm_kernel(x_ref, ctx_ref, modb_ref, modc_ref, g_ref, xs_ref, h_ref, *, n_lat):
    rows = x_ref.shape[0]
    j = pl.program_id(1)
    is_ctx = _ctx_rows(j, rows, n_lat)
    x = jnp.where(j * rows < n_lat, x_ref[...], ctx_ref[...])
    xs_ref[...] = x
    y = _rms(x) * g_ref[...]
    h = y * (1.0 + _mod(modb_ref, modc_ref, 1, is_ctx)) + _mod(modb_ref, modc_ref, 0, is_ctx)
    h_ref[...] = h.astype(h_ref.dtype)


def _mod_specs(d, n_batch):
    return [
        pl.BlockSpec((None, N_MOD, d), lambda b, j: (b, 0, 0)),
        pl.BlockSpec((None, N_MOD, d), lambda b, j: (n_batch, 0, 0)),
    ]


def _prenorm(x, ctx, mod, g):
    nb, n_lat, d = x.shape
    n_ctx = ctx.shape[1]
    rows = math.gcd(n_lat, n_ctx)
    n_lat_blocks = n_lat // rows
    row = lambda b, j: (b, j, 0)
    block = pl.BlockSpec((None, rows, d), row)
    return pl.pallas_call(
        functools.partial(_prenorm_kernel, n_lat=n_lat),
        grid=(nb, (n_lat + n_ctx) // rows),
        in_specs=[
            pl.BlockSpec((None, rows, d), lambda b, j: (b, jnp.minimum(j, n_lat_blocks - 1), 0)),
            pl.BlockSpec((None, rows, d), lambda b, j: (b, jnp.maximum(j - n_lat_blocks, 0), 0)),
        ] + _mod_specs(d, nb) + [pl.BlockSpec((1, d), lambda b, j: (0, 0))],
        out_specs=[block, block],
        out_shape=[jax.ShapeDtypeStruct((nb, n_lat + n_ctx, d), F32),
                   jax.ShapeDtypeStruct((nb, n_lat + n_ctx, d), BF16)],
        compiler_params=_params("parallel", "parallel"),
        name="prenorm",
    )(x, ctx, mod, mod, g.reshape(1, d))


def _rope128(a, c, s):
    return a * c + pltpu.roll(a, LANES // 2, axis=1) * s


def _rms_heads(a):
    r = lax.broadcasted_iota(jnp.int32, (MXU_DIM, MXU_DIM), 0) // LANES
    c = lax.broadcasted_iota(jnp.int32, (MXU_DIM, MXU_DIM), 1) // LANES
    ones = (r == c).astype(BF16)
    sq = a * a
    hi = sq.astype(BF16)
    lo = (sq - hi.astype(F32)).astype(BF16)
    tot = (jnp.dot(hi, ones, preferred_element_type=F32)
           + jnp.dot(lo, ones, preferred_element_type=F32))
    return a * lax.rsqrt(tot * (1.0 / LANES) + EPS)


def _inproj_kernel(src_ref, h_ref, *refs):
    del src_ref
    n_pairs = IN_BLOCK_UNITS // IN_PAIR
    w_refs = refs[:n_pairs]
    c128_ref, s128_ref, c256_ref, s256_ref, qn_ref, kn_ref, o_ref, w_ref = refs[n_pairs:]
    j = pl.program_id(0)
    t = h_ref.shape[0]

    @pl.when(pl.program_id(1) == 0)
    def _():
        for p, wp_ref in enumerate(w_refs):
            w_ref[:, p * MXU_DIM:(p + 1) * MXU_DIM] = wp_ref[...].astype(w_ref.dtype)

    def for_pairs(fn):
        steps = [(pl.ds(ci * ROW_BLOCK, ROW_BLOCK), u)
                 for ci in range(t // ROW_BLOCK) for u in range(0, IN_BLOCK_UNITS, IN_PAIR)]
        pending = None
        for rows, u in steps + [(None, None)]:
            if rows is not None:
                cols = slice(u * UNIT, (u + IN_PAIR) * UNIT)
                a = jnp.dot(h_ref[rows, :], w_ref[:, cols], preferred_element_type=F32)
            if pending is not None:
                p_rows, p_u, p_a = pending
                o_ref[p_rows, p_u * UNIT:(p_u + IN_PAIR) * UNIT] = fn(p_u, p_a, p_rows).astype(o_ref.dtype)
            pending = (rows, u, a) if rows is not None else None

    def rope_heads(a, rows, gain=None, scale=None):
        out = []
        for a1 in (a[:, :UNIT], a[:, UNIT:]):
            if gain is not None:
                a1 = a1 * gain
            a1 = _rope128(a1, c128_ref[rows, :], s128_ref[rows, :])
            out.append(a1 if scale is None else a1 * scale)
        return jnp.concatenate(out, axis=1)

    @pl.when(j == U_AQ // IN_BLOCK_UNITS)
    def _():
        scale = A_HD ** -0.5 * LOG2E
        for_pairs(lambda u, a, rows: rope_heads(_rms_heads(a), rows, qn_ref[...], scale))

    @pl.when(j == U_BQ // IN_BLOCK_UNITS)
    def _():
        scale = B_HD ** -0.5 * LOG2E
        for_pairs(lambda u, a, rows: rope_heads(a, rows, scale=scale))

    @pl.when(j == U_AK // IN_BLOCK_UNITS)
    def _():
        def fn(u, a, rows):
            if u < U_BK - U_AK:
                return rope_heads(_rms_heads(a), rows, kn_ref[...])
            if u < U_AV - U_AK:
                return rope_heads(a, rows)
            return a
        for_pairs(fn)

    def rope_r(scale):
        def fn(u, a, rows):
            a0, a1 = a[:, :UNIT], a[:, UNIT:]
            c, s = c256_ref[rows, :], s256_ref[rows, :]
            return jnp.concatenate([(a0 * c - a1 * s) * scale, (a0 * s + a1 * c) * scale], axis=1)
        return fn

    @pl.when(j == U_CQ // IN_BLOCK_UNITS)
    def _():
        for_pairs(rope_r(1.0))

    @pl.when(j == U_CK // IN_BLOCK_UNITS)
    def _():
        for_pairs(rope_r(C_DK ** -0.5))

    @pl.when(j >= U_CV // IN_BLOCK_UNITS)
    def _():
        for_pairs(lambda u, a, rows: a)


def _inproj(h, w_in, layer, tabs, qn, kn):
    nb, t, d = h.shape
    width = w_in.shape[2]
    tn = IN_BLOCK_UNITS * UNIT
    n_pairs = IN_BLOCK_UNITS // IN_PAIR
    assert all(a % IN_PAIR == 0 and b == a + 1 for a, b in zip(_UNIT_PERM[0::2], _UNIT_PERM[1::2]))
    src = jnp.asarray([a // IN_PAIR for a in _UNIT_PERM[0::IN_PAIR]], jnp.int32)
    tab = pl.BlockSpec((t, LANES), lambda j, b, src: (0, 0))
    gain = pl.BlockSpec((1, LANES), lambda j, b, src: (0, 0))

    def pair(p):
        return pl.BlockSpec((None, d, MXU_DIM), lambda j, b, src: (layer, 0, src[j * n_pairs + p]))

    return pl.pallas_call(
        _inproj_kernel,
        grid_spec=pltpu.PrefetchScalarGridSpec(
            num_scalar_prefetch=1,
            grid=(width // tn, nb),
            in_specs=[pl.BlockSpec((None, t, d), lambda j, b, src: (b, 0, 0))]
            + [pair(p) for p in range(n_pairs)] + [tab, tab, tab, tab, gain, gain],
            out_specs=pl.BlockSpec((None, t, tn), lambda j, b, src: (b, 0, j)),
            scratch_shapes=[pltpu.VMEM((d, tn), BF16)],
        ),
        out_shape=jax.ShapeDtypeStruct((nb, t, width), BF16),
        compiler_params=_params("arbitrary", "arbitrary"),
        name="inproj",
    )(src, h, *([w_in] * n_pairs), *tabs, qn.reshape(1, LANES), kn.reshape(1, LANES))


def _scores_t(k, q):
    return lax.dot_general(k, q, (((1,), (1,)), ((), ())), preferred_element_type=F32)


def _stack_heads(q_ref, group, hd):
    return jnp.concatenate([q_ref[:, g * hd:(g + 1) * hd] for g in range(group)], axis=0)


def _store_heads(o_ref, o, group, hd):
    for g in range(group):
        o_ref[:, g * hd:(g + 1) * hd] = o[g * TQ:(g + 1) * TQ, :].astype(o_ref.dtype)


class _Stream(NamedTuple):
    q_ref: Any
    o_ref: Any
    s_scr: Any
    m_scr: Any
    lat_chunks: Any
    ctx_chunks: Any
    keys: Callable
    values_t: Callable
    bias: Callable
    sink: Any


def _softmax_pipeline(i, n_q_lat, streams, group, hd):
    def rows(c):
        return pl.ds(c * KCH, KCH)

    def score_chunk(st, q, c, m):
        s = _scores_t(st.keys(c, i), q)
        b = st.bias(c, i)
        if b is not None:
            s = jnp.concatenate([s[:, g * TQ:(g + 1) * TQ] + b for g in range(group)], axis=1)
        st.s_scr[rows(c), :] = s
        mc = jnp.max(s, axis=0, keepdims=True)
        return mc if m is None else jnp.maximum(m, mc)

    def value_chunk(st, c, m, den, o):
        p = jnp.exp2(st.s_scr[rows(c), :] - m)
        dc = jnp.sum(p, axis=0, keepdims=True)
        oc = jnp.dot(st.values_t(c, i - 1), p.astype(BF16), preferred_element_type=F32)
        return (dc if den is None else den + dc), (oc if o is None else o + oc)

    def run(score_kind, value_kind):
        work, state = [], []
        for si, st in enumerate(streams):
            sc = set(getattr(st, score_kind)) if score_kind else set()
            vc = set(getattr(st, value_kind)) if value_kind else set()
            m_old = st.m_scr[0:1, :] if vc else None
            den = jnp.exp2(st.sink - m_old) if vc and st.sink is not None else None
            state.append(dict(sc=sc, vc=vc, m_old=m_old, den=den, o=None, m_new=st.sink,
                              q=_stack_heads(st.q_ref, group, hd) if sc else None))
            chunks = sorted(sc | vc)
            work += [((k + 0.5) / len(chunks), si, c) for k, c in enumerate(chunks)]
        for _, si, c in sorted(work):
            st, z = streams[si], state[si]
            if c in z["vc"]:
                z["den"], z["o"] = value_chunk(st, c, z["m_old"], z["den"], z["o"])
            if c in z["sc"]:
                z["m_new"] = score_chunk(st, z["q"], c, z["m_new"])
        for st, z in zip(streams, state):
            if z["vc"]:
                _store_heads(st.o_ref, (z["o"] * (1.0 / z["den"])).T, group, hd)
            if z["sc"]:
                st.m_scr[0:1, :] = z["m_new"]

    @pl.when(i == 0)
    def _():
        run("lat_chunks", None)

    @pl.when((i > 0) & (i < n_q_lat))
    def _():
        run("lat_chunks", "lat_chunks")

    @pl.when(i == n_q_lat)
    def _():
        run("ctx_chunks", "lat_chunks")

    @pl.when(i > n_q_lat)
    def _():
        run(None, "ctx_chunks")


def _attn_kernel(sink_ref, qa_ref, ka_ref, va_ref, qb_ref, kb_ref, vb_ref, oa_ref, ob_ref,
                 vt_scr, sa_scr, ma_scr, band_scr, sb_scr, mb_scr, *, n_lat, group, hd):
    kv = pl.program_id(1)
    i = pl.program_id(2)
    t = ka_ref.shape[0]
    span = TQ + 2 * WINDOW
    sink = jnp.concatenate(
        [jnp.full((1, TQ), sink_ref[kv * group + g], F32) for g in range(group)], axis=1) * LOG2E

    @pl.when(i == 0)
    def _():
        vt_scr[...] = va_ref[...].T
        r = lax.broadcasted_iota(jnp.int32, (span, TQ), 0)
        c = lax.broadcasted_iota(jnp.int32, (span, TQ), 1)
        for lead in range(band_scr.shape[0]):
            band_scr[lead] = jnp.where(jnp.abs(c + lead * WINDOW - r) <= WINDOW, 0.0, NEG)

    def start(blk):
        return jnp.clip(blk * TQ - WINDOW, 0, n_lat - span)

    def key_rows(c, blk):
        if c == 0:
            return pl.ds(n_lat, KCH)
        return pl.ds(pl.multiple_of(start(blk) + (c - 1) * KCH, WINDOW), KCH)

    def band(c, blk):
        if c == 0:
            return None
        return band_scr[(blk * TQ - start(blk)) // WINDOW, pl.ds((c - 1) * KCH, KCH), :]

    mixer_a = _Stream(
        qa_ref, oa_ref, sa_scr, ma_scr,
        lat_chunks=range(t // KCH), ctx_chunks=range(n_lat // KCH, t // KCH),
        keys=lambda c, blk: ka_ref[pl.ds(c * KCH, KCH), :],
        values_t=lambda c, blk: vt_scr[:, pl.ds(c * KCH, KCH)],
        bias=lambda c, blk: None, sink=None)
    mixer_b = _Stream(
        qb_ref, ob_ref, sb_scr, mb_scr,
        lat_chunks=range(1 + span // KCH), ctx_chunks=range(1),
        keys=lambda c, blk: kb_ref[key_rows(c, blk), :],
        values_t=lambda c, blk: vb_ref[key_rows(c, blk), :].T,
        bias=band, sink=sink)
    _softmax_pipeline(i, n_lat // TQ, (mixer_a, mixer_b), group, hd)


def _attention(sink, u, n_lat):
    nb, t, _ = u.shape
    assert (A_HEADS, A_KV, A_HD) == (B_HEADS, B_KV, B_HD), "the mixers share one grid"
    group = A_HEADS // A_KV
    hd = A_HD
    qw = group * hd
    n_q = t // TQ
    span = TQ + 2 * WINDOW
    assert n_q == n_lat // TQ + 1 and t - n_lat == KCH and span % KCH == 0

    def q_spec(unit):
        return pl.BlockSpec((None, TQ, qw),
                            lambda b, kv, i: (b, jnp.minimum(i, n_q - 1), unit * UNIT // qw + kv))

    def kv_spec(unit):
        return pl.BlockSpec((None, t, hd), lambda b, kv, i: (b, 0, unit + kv))

    out_spec = pl.BlockSpec((None, TQ, qw), lambda b, kv, i: (b, jnp.maximum(i - 1, 0), kv))
    out = jax.ShapeDtypeStruct((nb, t, A_HEADS * hd), BF16)
    return pl.pallas_call(
        functools.partial(_attn_kernel, n_lat=n_lat, group=group, hd=hd),
        grid=(nb, A_KV, n_q + 1),
        in_specs=[pl.BlockSpec(memory_space=pltpu.SMEM),
                  q_spec(U_AQ), kv_spec(U_AK), kv_spec(U_AV),
                  q_spec(U_BQ), kv_spec(U_BK), kv_spec(U_BV)],
        out_specs=[out_spec, out_spec],
        out_shape=[out, out],
        scratch_shapes=[
            pltpu.VMEM((hd, t), BF16),
            pltpu.VMEM((t, group * TQ), F32),
            pltpu.VMEM((8, group * TQ), F32),
            pltpu.VMEM((3, span, TQ), F32),
            pltpu.VMEM((KCH + span, group * TQ), F32),
            pltpu.VMEM((8, group * TQ), F32),
        ],
        compiler_params=_params("parallel", "parallel", "arbitrary"),
        name="attention",
    )(sink, u, u, u, u, u, u)


def _scale_rows(x, v):
    return jnp.concatenate(
        [x[:, u * LANES:(u + 1) * LANES] * v for u in range(x.shape[1] // LANES)], axis=1)


def _retention_kernel(logit_ref, q_ref, k_ref, v_ref, g_ref, gn_ref, o_ref,
                      part_scr, state_scr, decay_scr, vec_scr, *, n_lat):
    h = pl.program_id(1)
    t = q_ref.shape[0]
    n_l, n_c = n_lat // RCH, (t - n_lat) // RCH
    n = n_l + n_c
    order = ([n_l + c for c in range(n_c)] + list(range(n_l)),
             [n_l + c for c in reversed(range(n_c))] + list(reversed(range(n_l))))
    when = tuple({c: i for i, c in enumerate(o)} for o in order)

    def log_gamma(direction, width):
        x = jnp.full((1, width), logit_ref[direction, h], F32)
        return jnp.minimum(x, 0.0) - jnp.log1p(jnp.exp(-jnp.abs(x)))

    qi = lax.broadcasted_iota(jnp.int32, (RCH, RCH), 0).astype(F32)
    si = lax.broadcasted_iota(jnp.int32, (RCH, RCH), 1).astype(F32)
    ri = lax.broadcasted_iota(jnp.int32, (RCH, LANES), 0).astype(F32)
    carry = []
    for d in range(2):
        lag = qi - si if d == 0 else si - qi
        decay_scr[d] = jnp.where(lag >= 0, jnp.exp(jnp.maximum(lag, 0.0) * log_gamma(d, RCH)), 0.0)
        lg = log_gamma(d, LANES)
        vec_scr[2 * d] = jnp.exp((ri + 1.0 if d == 0 else RCH - ri) * lg)
        vec_scr[2 * d + 1] = jnp.exp((RCH - 1.0 - ri if d == 0 else ri) * lg)
        carry.append(jnp.exp(RCH * log_gamma(d, C_DV)))

    def step(d, i):
        rows = pl.ds(order[d][i] * RCH, RCH)
        qb, kb, vb = q_ref[rows, :], k_ref[rows, :], v_ref[rows, :]
        att = _scores_t(qb, kb) * decay_scr[d]
        o = jnp.dot(att.astype(BF16), vb, preferred_element_type=F32)
        if i > 0:
            qd = _scale_rows(qb.astype(F32), vec_scr[2 * d]).astype(BF16)
            o = o + jnp.dot(qd, state_scr[d].astype(BF16), preferred_element_type=F32)
        if i < n - 1:
            kd = _scale_rows(kb.astype(F32), vec_scr[2 * d + 1]).astype(BF16)
            upd = lax.dot_general(kd, vb, (((0,), (0,)), ((), ())), preferred_element_type=F32)
            state_scr[d] = upd if i == 0 else state_scr[d] * carry[d] + upd
        return o

    def finish(c, o):
        rows = pl.ds(c * RCH, RCH)
        gate = g_ref[rows, :].astype(F32)
        y = _rms(o) * gn_ref[...] * (gate * _sigmoid(gate))
        o_ref[rows, :] = y.astype(o_ref.dtype)

    for i in range(n):
        outs = [step(d, i) for d in range(2)]
        if order[0][i] == order[1][i]:
            finish(order[0][i], outs[0] + outs[1])
            continue
        for d in range(2):
            c = order[d][i]
            rows = pl.ds(c * RCH, RCH)
            if when[1 - d][c] < i:
                finish(c, outs[d] + part_scr[rows, :])
            else:
                part_scr[rows, :] = outs[d]


def _retention(logit, u, gn, n_lat):
    nb, t, _ = u.shape
    return pl.pallas_call(
        functools.partial(_retention_kernel, n_lat=n_lat),
        grid=(nb, C_HEADS),
        in_specs=[
            pl.BlockSpec(memory_space=pltpu.SMEM),
            pl.BlockSpec((None, t, C_DK), lambda b, h: (b, 0, U_CQ * UNIT // C_DK + h)),
            pl.BlockSpec((None, t, C_DK), lambda b, h: (b, 0, U_CK * UNIT // C_DK + h)),
            pl.BlockSpec((None, t, C_DV), lambda b, h: (b, 0, U_CV * UNIT // C_DV + h)),
            pl.BlockSpec((None, t, C_DV), lambda b, h: (b, 0, U_CG * UNIT // C_DV + h)),
            pl.BlockSpec((1, C_DV), lambda b, h: (0, h)),
        ],
        out_specs=pl.BlockSpec((None, t, C_DV), lambda b, h: (b, 0, h)),
        out_shape=jax.ShapeDtypeStruct((nb, t, C_HEADS * C_DV), BF16),
        scratch_shapes=[
            pltpu.VMEM((t, C_DV), F32),
            pltpu.VMEM((2, C_DK, C_DV), F32),
            pltpu.VMEM((2, RCH, RCH), F32),
            pltpu.VMEM((4, RCH, LANES), F32),
        ],
        compiler_params=_params("parallel", "parallel"),
        name="retention",
    )(logit, u, u, u, u, gn.reshape(1, C_HEADS * C_DV))


def _merge_kernel(ya_ref, yb_ref, yc_ref, ga_ref, gb_ref, gc_ref, x_ref, modb_ref, modc_ref, g2_ref,
                  wpa_ref, wpb_ref, wpc_ref, wo_ref, x_out_ref, h_out_ref, *, n_lat):
    rows = x_ref.shape[0]
    is_ctx = _ctx_rows(pl.program_id(1), rows, n_lat)

    def branch(y_ref, w_ref, gate_ref):
        p = jnp.dot(y_ref[...], w_ref[...], preferred_element_type=F32)
        return _sigmoid(gate_ref[...].astype(F32)) * p

    y = branch(ya_ref, wpa_ref, ga_ref) + branch(yb_ref, wpb_ref, gb_ref) + branch(yc_ref, wpc_ref, gc_ref)
    o = jnp.dot(y.astype(BF16), wo_ref[...], preferred_element_type=F32)
    x = x_ref[...] + _mod(modb_ref, modc_ref, 2, is_ctx) * o
    x_out_ref[...] = x
    h = _rms(x) * g2_ref[...]
    h = h * (1.0 + _mod(modb_ref, modc_ref, 4, is_ctx)) + _mod(modb_ref, modc_ref, 3, is_ctx)
    h_out_ref[...] = h.astype(h_out_ref.dtype)


def _layer_weight(w, layer, **kw):
    return pl.BlockSpec((None,) + w.shape[1:], lambda b, j: (layer, 0, 0), **kw)


def _merge(ya, yb, yc, u, x, mod, g2, wpa, wpb, wpc, wo, layer, n_lat):
    nb, t, d = x.shape
    row = lambda b, j: (b, j, 0)
    full = lambda b, j: (0, 0)
    gate = lambda unit: pl.BlockSpec((None, ROW_BLOCK, d), lambda b, j: (b, j, unit * UNIT // d))
    return pl.pallas_call(
        functools.partial(_merge_kernel, n_lat=n_lat),
        grid=(nb, t // ROW_BLOCK),
        in_specs=[
            pl.BlockSpec((None, ROW_BLOCK, ya.shape[2]), row),
            pl.BlockSpec((None, ROW_BLOCK, yb.shape[2]), row),
            pl.BlockSpec((None, ROW_BLOCK, yc.shape[2]), row),
            gate(U_GA), gate(U_GB), gate(U_GC),
            pl.BlockSpec((None, ROW_BLOCK, d), row),
        ] + _mod_specs(d, nb) + [
            pl.BlockSpec((1, d), full),
            _layer_weight(wpa, layer), _layer_weight(wpb, layer),
            _layer_weight(wpc, layer), _layer_weight(wo, layer),
        ],
        out_specs=[pl.BlockSpec((None, ROW_BLOCK, d), row), pl.BlockSpec((None, ROW_BLOCK, d), row)],
        out_shape=[jax.ShapeDtypeStruct((nb, t, d), F32), jax.ShapeDtypeStruct((nb, t, d), BF16)],
        compiler_params=_params("parallel", "parallel"),
        name="merge",
    )(ya, yb, yc, u, u, u, x, mod, mod, g2.reshape(1, d), wpa, wpb, wpc, wo)


def _mlp_kernel(h_ref, x_ref, modb_ref, modc_ref, nmodb_ref, nmodc_ref, gn_ref, w1_ref, w2_ref,
                *out_refs, n_lat, ff_block, final):
    rows = x_ref.shape[0]
    is_ctx = _ctx_rows(pl.program_id(1), rows, n_lat)
    h = h_ref[...]
    acc = jnp.zeros(x_ref.shape, F32)
    for f in range(w1_ref.shape[1] // ff_block):
        cols = slice(f * ff_block, (f + 1) * ff_block)
        a = jnp.maximum(jnp.dot(h, w1_ref[:, cols], preferred_element_type=F32), 0.0)
        acc = acc + jnp.dot((a * a).astype(BF16), w2_ref[cols, :], preferred_element_type=F32)
    x = x_ref[...] + _mod(modb_ref, modc_ref, 5, is_ctx) * acc
    y = _rms(x) * gn_ref[...]
    if final:
        out_refs[0][...] = y
    else:
        out_refs[0][...] = x
        hn = y * (1.0 + _mod(nmodb_ref, nmodc_ref, 1, is_ctx)) + _mod(nmodb_ref, nmodc_ref, 0, is_ctx)
        out_refs[1][...] = hn.astype(out_refs[1].dtype)


def _mlp(h, x, mod, next_mod, gn, w1, w2, layer, n_lat, final):
    nb, t, d = x.shape
    row = lambda b, j: (b, j, 0)
    full = lambda b, j: (0, 0)
    block = pl.BlockSpec((None, ROW_BLOCK, d), row)
    if final:
        out_specs = [block]
        out_shape = [jax.ShapeDtypeStruct((nb, n_lat, d), F32)]
    else:
        out_specs = [block, block]
        out_shape = [jax.ShapeDtypeStruct((nb, t, d), F32), jax.ShapeDtypeStruct((nb, t, d), BF16)]
    return pl.pallas_call(
        functools.partial(_mlp_kernel, n_lat=n_lat, ff_block=1024, final=final),
        grid=(nb, t // ROW_BLOCK),
        in_specs=[block, block] + _mod_specs(d, nb) + _mod_specs(d, nb) + [
            pl.BlockSpec((1, d), full),
            _layer_weight(w1, layer, pipeline_mode=pl.Buffered(1)),
            _layer_weight(w2, layer, pipeline_mode=pl.Buffered(1)),
        ],
        out_specs=out_specs,
        out_shape=out_shape,
        compiler_params=_params("parallel", "parallel"),
        name="mlp",
    )(h, x, mod, mod, next_mod, next_mod, gn.reshape(1, d), w1, w2)


def _rope_tables(n_lat, n_ctx):
    rows = n_lat // GRID_W
    row = jnp.arange(rows).astype(F32)
    col = jnp.arange(GRID_W).astype(F32)

    def cos_sin(dim):
        n = dim // 4
        inv = ROPE_THETA ** (-jnp.arange(n, dtype=F32) / n)
        ang_r, ang_c = row[:, None] * inv, col[:, None] * inv
        spread = lambda fn: jnp.concatenate(
            [jnp.repeat(fn(ang_r), GRID_W, axis=0), jnp.tile(fn(ang_c), (rows, 1))], axis=-1)
        return spread(jnp.cos), spread(jnp.sin)

    def with_ctx(tab, fill):
        return jnp.concatenate([tab, jnp.full((n_ctx, tab.shape[1]), fill, F32)], axis=0)

    cos_h, sin_h = cos_sin(A_HD)
    cos_r, sin_r = cos_sin(C_DK)
    return (
        with_ctx(jnp.concatenate([cos_h, cos_h], axis=-1), 1.0),
        with_ctx(jnp.concatenate([-sin_h, sin_h], axis=-1), 0.0),
        with_ctx(cos_r, 1.0),
        with_ctx(sin_r, 0.0),
    )


def kernel(x, c, ctx, c_ctx, w_ada, b_ada, norm1_g, norm2_g, w_in, qn_g, kn_g, sink, ret_logit, ret_norm_g,
           w_pa, w_pb, w_pc, w_o, w_ff1, w_ff2, final_g):
    nb, n_lat, d = x.shape
    n_ctx = ctx.shape[1]
    depth = w_ada.shape[0]
    t = n_lat + n_ctx
    assert t % ROW_BLOCK == 0 and n_lat % TQ == 0 and n_ctx % TQ == 0 and n_lat % GRID_W == 0
    assert n_lat >= TQ + 2 * WINDOW and w_in.shape[2] == N_UNITS * UNIT

    tabs = _rope_tables(n_lat, n_ctx)
    mod_rows = 16
    craw = jnp.concatenate([c, c_ctx[None], jnp.zeros((mod_rows - nb - 1, d), F32)], axis=0)
    mod = _ada(craw, w_ada, b_ada).reshape(depth, mod_rows, N_MOD, d)

    merge_w = [w.astype(BF16) for w in (w_pa, w_pb, w_pc, w_o)]
    mlp_w = [w.astype(BF16) for w in (w_ff1, w_ff2)]

    xs, h = _prenorm(x, ctx, mod[0], norm1_g[0])
    for l in range(depth):
        u = _inproj(h, w_in, l, tabs, qn_g[l], kn_g[l])
        ya, yb = _attention(sink[l], u, n_lat)
        yc = _retention(ret_logit[l], u, ret_norm_g[l], n_lat)
        xs, h2 = _merge(ya, yb, yc, u, xs, mod[l], norm2_g[l], *merge_w, l, n_lat)
        final = l == depth - 1
        gn = final_g if final else norm1_g[l + 1]
        outs = _mlp(h2, xs, mod[l], mod[l if final else l + 1], gn, *mlp_w, l, n_lat, final)
        if final:
            return outs[0]
        xs, h = outs
```

```python
import functools
import math
from typing import Any, Callable, NamedTuple

import jax
import jax.numpy as jnp
from jax import lax
from jax.experimental import pallas as pl
from jax.experimental.pallas import tpu as pltpu

F32 = jnp.float32
BF16 = jnp.bfloat16

GRID_W = 64
WINDOW = 128
ROPE_THETA = 10000.0
EPS = 1e-6
NEG = -1e30
A_HEADS, A_KV, A_HD = 8, 2, 128
B_HEADS, B_KV, B_HD = 8, 2, 128
C_HEADS, C_DK, C_DV = 4, 256, 512
N_MOD = 6
LOG2E = math.log2(math.e)

LANES = 128
MXU_DIM = 256
V7X_VMEM_BYTES = 64 * 1024 * 1024
VMEM_LIMIT = V7X_VMEM_BYTES - 8 * 1024 * 1024

ROW_BLOCK = 768
LATENT_ROW_BLOCK = 512
TQ = MXU_DIM
KCH = MXU_DIM
RCH = MXU_DIM
UNIT = LANES

U_AQ, U_BQ = 0, 8
U_AK, U_BK, U_AV, U_BV = 16, 18, 20, 22
U_CQ, U_CK, U_CV, U_CG = 24, 32, 40, 56
U_GA, U_GB, U_GC = 72, 80, 88
N_UNITS = 96
_UNIT_PERM = (list(range(0, 8)) + list(range(12, 20)) + [8, 9, 20, 21, 10, 11, 22, 23]
              + list(range(24, 96)))
IN_BLOCK_UNITS = 8
IN_PAIR = MXU_DIM // UNIT


def _params(*sem):
    return pltpu.CompilerParams(dimension_semantics=sem, vmem_limit_bytes=VMEM_LIMIT)


def _rms(x):
    return x * lax.rsqrt(jnp.mean(x * x, axis=-1, keepdims=True) + EPS)


def _sigmoid(x):
    return 1.0 / (1.0 + jnp.exp2(x * -LOG2E))


def _ctx_rows(j, rows, n_lat):
    r = j * rows + lax.broadcasted_iota(jnp.int32, (rows, 1), 0)
    return r >= n_lat


def _mod(modb_ref, modc_ref, idx, is_ctx):
    return jnp.where(is_ctx, modc_ref[idx:idx + 1, :], modb_ref[idx:idx + 1, :])


def _ada_kernel(c_ref, w_ref, b_ref, o_ref):
    c = c_ref[...]
    s = (c * _sigmoid(c)).astype(BF16)
    o_ref[...] = jnp.dot(s, w_ref[...].astype(BF16), preferred_element_type=F32) + b_ref[...]


def _ada(craw, w_ada, b_ada):
    depth, d, width = w_ada.shape
    rows = craw.shape[0]
    tn = 1536
    return pl.pallas_call(
        _ada_kernel,
        grid=(depth, width // tn),
        in_specs=[
            pl.BlockSpec((rows, d), lambda l, j: (0, 0)),
            pl.BlockSpec((None, d, tn), lambda l, j: (l, 0, j)),
            pl.BlockSpec((None, 1, tn), lambda l, j: (l, 0, j)),
        ],
        out_specs=pl.BlockSpec((None, rows, tn), lambda l, j: (l, 0, j)),
        out_shape=jax.ShapeDtypeStruct((depth, rows, width), F32),
        compiler_params=_params("parallel", "parallel"),
        name="ada",
    )(craw, w_ada, b_ada.reshape(depth, 1, width))


def _prenorm_kernel(x_ref, ctx_ref, modb_ref, modc_ref, g_ref, xs_ref, h_ref, *, n_lat):
    rows = x_ref.shape[0]
    j = pl.program_id(1)
    is_ctx = _ctx_rows(j, rows, n_lat)
    x = jnp.where(j * rows < n_lat, x_ref[...], ctx_ref[...])
    xs_ref[...] = x
    y = _rms(x) * g_ref[...]
    h = y * (1.0 + _mod(modb_ref, modc_ref, 1, is_ctx)) + _mod(modb_ref, modc_ref, 0, is_ctx)
    h_ref[...] = h.astype(h_ref.dtype)


def _mod_specs(d, n_batch):
    return [
        pl.BlockSpec((None, N_MOD, d), lambda b, j: (b, 0, 0)),
        pl.BlockSpec((None, N_MOD, d), lambda b, j: (n_batch, 0, 0)),
    ]


def _prenorm(x, ctx, mod, g):
    nb, n_lat, d = x.shape
    n_ctx = ctx.shape[1]
    rows = math.gcd(n_lat, n_ctx)
    n_lat_blocks = n_lat // rows
    row = lambda b, j: (b, j, 0)
    block = pl.BlockSpec((None, rows, d), row)
    return pl.pallas_call(
        functools.partial(_prenorm_kernel, n_lat=n_lat),
        grid=(nb, (n_lat + n_ctx) // rows),
        in_specs=[
            pl.BlockSpec((None, rows, d), lambda b, j: (b, jnp.minimum(j, n_lat_blocks - 1), 0)),
            pl.BlockSpec((None, rows, d), lambda b, j: (b, jnp.maximum(j - n_lat_blocks, 0), 0)),
        ] + _mod_specs(d, nb) + [pl.BlockSpec((1, d), lambda b, j: (0, 0))],
        out_specs=[block, block],
        out_shape=[jax.ShapeDtypeStruct((nb, n_lat + n_ctx, d), F32),
                   jax.ShapeDtypeStruct((nb, n_lat + n_ctx, d), BF16)],
        compiler_params=_params("parallel", "parallel"),
        name="prenorm",
    )(x, ctx, mod, mod, g.reshape(1, d))


def _rope128(a, c, s):
    return a * c + pltpu.roll(a, LANES // 2, axis=1) * s


def _rms_heads(a):
    r = lax.broadcasted_iota(jnp.int32, (MXU_DIM, MXU_DIM), 0) // LANES
    c = lax.broadcasted_iota(jnp.int32, (MXU_DIM, MXU_DIM), 1) // LANES
    ones = (r == c).astype(BF16)
    sq = a * a
    hi = sq.astype(BF16)
    lo = (sq - hi.astype(F32)).astype(BF16)
    tot = (jnp.dot(hi, ones, preferred_element_type=F32)
           + jnp.dot(lo, ones, preferred_element_type=F32))
    return a * lax.rsqrt(tot * (1.0 / LANES) + EPS)


def _inproj_kernel(src_ref, h_ref, *refs):
    del src_ref
    n_pairs = IN_BLOCK_UNITS // IN_PAIR
    w_refs = refs[:n_pairs]
    c128_ref, s128_ref, c256_ref, s256_ref, qn_ref, kn_ref, o_ref, w_ref = refs[n_pairs:]
    j = pl.program_id(0)
    t = h_ref.shape[0]

    @pl.when(pl.program_id(1) == 0)
    def _():
        for p, wp_ref in enumerate(w_refs):
            w_ref[:, p * MXU_DIM:(p + 1) * MXU_DIM] = wp_ref[...].astype(w_ref.dtype)

    def for_pairs(fn):
        steps = [(pl.ds(ci * ROW_BLOCK, ROW_BLOCK), u)
                 for ci in range(t // ROW_BLOCK) for u in range(0, IN_BLOCK_UNITS, IN_PAIR)]
        pending = None
        for rows, u in steps + [(None, None)]:
            if rows is not None:
                cols = slice(u * UNIT, (u + IN_PAIR) * UNIT)
                a = jnp.dot(h_ref[rows, :], w_ref[:, cols], preferred_element_type=F32)
            if pending is not None:
                p_rows, p_u, p_a = pending
                o_ref[p_rows, p_u * UNIT:(p_u + IN_PAIR) * UNIT] = fn(p_u, p_a, p_rows).astype(o_ref.dtype)
            pending = (rows, u, a) if rows is not None else None

    def rope_heads(a, rows, gain=None, scale=None):
        out = []
        for a1 in (a[:, :UNIT], a[:, UNIT:]):
            if gain is not None:
                a1 = a1 * gain
            a1 = _rope128(a1, c128_ref[rows, :], s128_ref[rows, :])
            out.append(a1 if scale is None else a1 * scale)
        return jnp.concatenate(out, axis=1)

    @pl.when(j == U_AQ // IN_BLOCK_UNITS)
    def _():
        scale = A_HD ** -0.5 * LOG2E
        for_pairs(lambda u, a, rows: rope_heads(_rms_heads(a), rows, qn_ref[...], scale))

    @pl.when(j == U_BQ // IN_BLOCK_UNITS)
    def _():
        scale = B_HD ** -0.5 * LOG2E
        for_pairs(lambda u, a, rows: rope_heads(a, rows, scale=scale))

    @pl.when(j == U_AK // IN_BLOCK_UNITS)
    def _():
        def fn(u, a, rows):
            if u < U_BK - U_AK:
                return rope_heads(_rms_heads(a), rows, kn_ref[...])
            if u < U_AV - U_AK:
                return rope_heads(a, rows)
            return a
        for_pairs(fn)

    def rope_r(scale):
        def fn(u, a, rows):
            a0, a1 = a[:, :UNIT], a[:, UNIT:]
            c, s = c256_ref[rows, :], s256_ref[rows, :]
            return jnp.concatenate([(a0 * c - a1 * s) * scale, (a0 * s + a1 * c) * scale], axis=1)
        return fn

    @pl.when(j == U_CQ // IN_BLOCK_UNITS)
    def _():
        for_pairs(rope_r(1.0))

    @pl.when(j == U_CK // IN_BLOCK_UNITS)
    def _():
        for_pairs(rope_r(C_DK ** -0.5))

    @pl.when(j >= U_CV // IN_BLOCK_UNITS)
    def _():
        for_pairs(lambda u, a, rows: a)


def _inproj(h, w_in, layer, tabs, qn, kn):
    nb, t, d = h.shape
    width = w_in.shape[2]
    tn = IN_BLOCK_UNITS * UNIT
    n_pairs = IN_BLOCK_UNITS // IN_PAIR
    assert all(a % IN_PAIR == 0 and b == a + 1 for a, b in zip(_UNIT_PERM[0::2], _UNIT_PERM[1::2]))
    src = jnp.asarray([a // IN_PAIR for a in _UNIT_PERM[0::IN_PAIR]], jnp.int32)
    tab = pl.BlockSpec((t, LANES), lambda j, b, src: (0, 0))
    gain = pl.BlockSpec((1, LANES), lambda j, b, src: (0, 0))

    def pair(p):
        return pl.BlockSpec((None, d, MXU_DIM), lambda j, b, src: (layer, 0, src[j * n_pairs + p]))

    return pl.pallas_call(
        _inproj_kernel,
        grid_spec=pltpu.PrefetchScalarGridSpec(
            num_scalar_prefetch=1,
            grid=(width // tn, nb),
            in_specs=[pl.BlockSpec((None, t, d), lambda j, b, src: (b, 0, 0))]
            + [pair(p) for p in range(n_pairs)] + [tab, tab, tab, tab, gain, gain],
            out_specs=pl.BlockSpec((None, t, tn), lambda j, b, src: (b, 0, j)),
            scratch_shapes=[pltpu.VMEM((d, tn), BF16)],
        ),
        out_shape=jax.ShapeDtypeStruct((nb, t, width), BF16),
        compiler_params=_params("arbitrary", "arbitrary"),
        name="inproj",
    )(src, h, *([w_in] * n_pairs), *tabs, qn.reshape(1, LANES), kn.reshape(1, LANES))


def _scores_t(k, q):
    return lax.dot_general(k, q, (((1,), (1,)), ((), ())), preferred_element_type=F32)


def _stack_heads(q_ref, group, hd):
    return jnp.concatenate([q_ref[:, g * hd:(g + 1) * hd] for g in range(group)], axis=0)


def _store_heads(o_ref, o, group, hd):
    for g in range(group):
        o_ref[:, g * hd:(g + 1) * hd] = o[g * TQ:(g + 1) * TQ, :].astype(o_ref.dtype)


class _Stream(NamedTuple):
    q_ref: Any
    o_ref: Any
    s_scr: Any
    m_scr: Any
    lat_chunks: Any
    ctx_chunks: Any
    keys: Callable
    values_t: Callable
    bias: Callable
    sink: Any


def _softmax_pipeline(i, n_q_lat, streams, group, hd):
    def rows(c):
        return pl.ds(c * KCH, KCH)

    def score_chunk(st, q, c, m):
        s = _scores_t(st.keys(c, i), q)
        b = st.bias(c, i)
        if b is not None:
            s = jnp.concatenate([s[:, g * TQ:(g + 1) * TQ] + b for g in range(group)], axis=1)
        st.s_scr[rows(c), :] = s
        mc = jnp.max(s, axis=0, keepdims=True)
        return mc if m is None else jnp.maximum(m, mc)

    def value_chunk(st, c, m, den, o):
        p = jnp.exp2(st.s_scr[rows(c), :] - m)
        dc = jnp.sum(p, axis=0, keepdims=True)
        oc = jnp.dot(st.values_t(c, i - 1), p.astype(BF16), preferred_element_type=F32)
        return (dc if den is None else den + dc), (oc if o is None else o + oc)

    def run(score_kind, value_kind):
        work, state = [], []
        for si, st in enumerate(streams):
            sc = set(getattr(st, score_kind)) if score_kind else set()
            vc = set(getattr(st, value_kind)) if value_kind else set()
            m_old = st.m_scr[0:1, :] if vc else None
            den = jnp.exp2(st.sink - m_old) if vc and st.sink is not None else None
            state.append(dict(sc=sc, vc=vc, m_old=m_old, den=den, o=None, m_new=st.sink,
                              q=_stack_heads(st.q_ref, group, hd) if sc else None))
            chunks = sorted(sc | vc)
            work += [((k + 0.5) / len(chunks), si, c) for k, c in enumerate(chunks)]
        for _, si, c in sorted(work):
            st, z = streams[si], state[si]
            if c in z["vc"]:
                z["den"], z["o"] = value_chunk(st, c, z["m_old"], z["den"], z["o"])
            if c in z["sc"]:
                z["m_new"] = score_chunk(st, z["q"], c, z["m_new"])
        for st, z in zip(streams, state):
            if z["vc"]:
                _store_heads(st.o_ref, (z["o"] * (1.0 / z["den"])).T, group, hd)
            if z["sc"]:
                st.m_scr[0:1, :] = z["m_new"]

    @pl.when(i == 0)
    def _():
        run("lat_chunks", None)

    @pl.when((i > 0) & (i < n_q_lat))
    def _():
        run("lat_chunks", "lat_chunks")

    @pl.when(i == n_q_lat)
    def _():
        run("ctx_chunks", "lat_chunks")

    @pl.when(i > n_q_lat)
    def _():
        run(None, "ctx_chunks")


def _attn_kernel(sink_ref, qa_ref, ka_ref, va_ref, qb_ref, kb_ref, vb_ref, oa_ref, ob_ref,
                 vt_scr, sa_scr, ma_scr, band_scr, sb_scr, mb_scr, *, n_lat, group, hd):
    kv = pl.program_id(1)
    i = pl.program_id(2)
    t = ka_ref.shape[0]
    span = TQ + 2 * WINDOW
    sink = jnp.concatenate(
        [jnp.full((1, TQ), sink_ref[kv * group + g], F32) for g in range(group)], axis=1) * LOG2E

    @pl.when(i == 0)
    def _():
        vt_scr[...] = va_ref[...].T
        r = lax.broadcasted_iota(jnp.int32, (span, TQ), 0)
        c = lax.broadcasted_iota(jnp.int32, (span, TQ), 1)
        for lead in range(band_scr.shape[0]):
            band_scr[lead] = jnp.where(jnp.abs(c + lead * WINDOW - r) <= WINDOW, 0.0, NEG)

    def start(blk):
        return jnp.clip(blk * TQ - WINDOW, 0, n_lat - span)

    def key_rows(c, blk):
        if c == 0:
            return pl.ds(n_lat, KCH)
        return pl.ds(pl.multiple_of(start(blk) + (c - 1) * KCH, WINDOW), KCH)

    def band(c, blk):
        if c == 0:
            return None
        return band_scr[(blk * TQ - start(blk)) // WINDOW, pl.ds((c - 1) * KCH, KCH), :]

    mixer_a = _Stream(
        qa_ref, oa_ref, sa_scr, ma_scr,
        lat_chunks=range(t // KCH), ctx_chunks=range(n_lat // KCH, t // KCH),
        keys=lambda c, blk: ka_ref[pl.ds(c * KCH, KCH), :],
        values_t=lambda c, blk: vt_scr[:, pl.ds(c * KCH, KCH)],
        bias=lambda c, blk: None, sink=None)
    mixer_b = _Stream(
        qb_ref, ob_ref, sb_scr, mb_scr,
        lat_chunks=range(1 + span // KCH), ctx_chunks=range(1),
        keys=lambda c, blk: kb_ref[key_rows(c, blk), :],
        values_t=lambda c, blk: vb_ref[key_rows(c, blk), :].T,
        bias=band, sink=sink)
    _softmax_pipeline(i, n_lat // TQ, (mixer_a, mixer_b), group, hd)


def _attention(sink, u, n_lat):
    nb, t, _ = u.shape
    assert (A_HEADS, A_KV, A_HD) == (B_HEADS, B_KV, B_HD), "the mixers share one grid"
    group = A_HEADS // A_KV
    hd = A_HD
    qw = group * hd
    n_q = t // TQ
    span = TQ + 2 * WINDOW
    assert n_q == n_lat // TQ + 1 and t - n_lat == KCH and span % KCH == 0

    def q_spec(unit):
        return pl.BlockSpec((None, TQ, qw),
                            lambda b, kv, i: (b, jnp.minimum(i, n_q - 1), unit * UNIT // qw + kv))

    def kv_spec(unit):
        return pl.BlockSpec((None, t, hd), lambda b, kv, i: (b, 0, unit + kv))

    out_spec = pl.BlockSpec((None, TQ, qw), lambda b, kv, i: (b, jnp.maximum(i - 1, 0), kv))
    out = jax.ShapeDtypeStruct((nb, t, A_HEADS * hd), BF16)
    return pl.pallas_call(
        functools.partial(_attn_kernel, n_lat=n_lat, group=group, hd=hd),
        grid=(nb, A_KV, n_q + 1),
        in_specs=[pl.BlockSpec(memory_space=pltpu.SMEM),
                  q_spec(U_AQ), kv_spec(U_AK), kv_spec(U_AV),
                  q_spec(U_BQ), kv_spec(U_BK), kv_spec(U_BV)],
        out_specs=[out_spec, out_spec],
        out_shape=[out, out],
        scratch_shapes=[
            pltpu.VMEM((hd, t), BF16),
            pltpu.VMEM((t, group * TQ), F32),
            pltpu.VMEM((8, group * TQ), F32),
            pltpu.VMEM((3, span, TQ), F32),
            pltpu.VMEM((KCH + span, group * TQ), F32),
            pltpu.VMEM((8, group * TQ), F32),
        ],
        compiler_params=_params("parallel", "parallel", "arbitrary"),
        name="attention",
    )(sink, u, u, u, u, u, u)


def _scale_rows(x, v):
    return jnp.concatenate(
        [x[:, u * LANES:(u + 1) * LANES] * v for u in range(x.shape[1] // LANES)], axis=1)


def _retention_kernel(logit_ref, q_ref, k_ref, v_ref, g_ref, gn_ref, o_ref,
                      part_scr, state_scr, decay_scr, vec_scr, *, n_lat):
    h = pl.program_id(1)
    t = q_ref.shape[0]
    n_l, n_c = n_lat // RCH, (t - n_lat) // RCH
    n = n_l + n_c
    order = ([n_l + c for c in range(n_c)] + list(range(n_l)),
             [n_l + c for c in reversed(range(n_c))] + list(reversed(range(n_l))))
    when = tuple({c: i for i, c in enumerate(o)} for o in order)

    def log_gamma(direction, width):
        x = jnp.full((1, width), logit_ref[direction, h], F32)
        return jnp.minimum(x, 0.0) - jnp.log1p(jnp.exp(-jnp.abs(x)))

    qi = lax.broadcasted_iota(jnp.int32, (RCH, RCH), 0).astype(F32)
    si = lax.broadcasted_iota(jnp.int32, (RCH, RCH), 1).astype(F32)
    ri = lax.broadcasted_iota(jnp.int32, (RCH, LANES), 0).astype(F32)
    carry = []
    for d in range(2):
        lag = qi - si if d == 0 else si - qi
        decay_scr[d] = jnp.where(lag >= 0, jnp.exp(jnp.maximum(lag, 0.0) * log_gamma(d, RCH)), 0.0)
        lg = log_gamma(d, LANES)
        vec_scr[2 * d] = jnp.exp((ri + 1.0 if d == 0 else RCH - ri) * lg)
        vec_scr[2 * d + 1] = jnp.exp((RCH - 1.0 - ri if d == 0 else ri) * lg)
        carry.append(jnp.exp(RCH * log_gamma(d, C_DV)))

    def step(d, i):
        rows = pl.ds(order[d][i] * RCH, RCH)
        qb, kb, vb = q_ref[rows, :], k_ref[rows, :], v_ref[rows, :]
        att = _scores_t(qb, kb) * decay_scr[d]
        o = jnp.dot(att.astype(BF16), vb, preferred_element_type=F32)
        if i > 0:
            qd = _scale_rows(qb.astype(F32), vec_scr[2 * d]).astype(BF16)
            o = o + jnp.dot(qd, state_scr[d].astype(BF16), preferred_element_type=F32)
        if i < n - 1:
            kd = _scale_rows(kb.astype(F32), vec_scr[2 * d + 1]).astype(BF16)
            upd = lax.dot_general(kd, vb, (((0,), (0,)), ((), ())), preferred_element_type=F32)
            state_scr[d] = upd if i == 0 else state_scr[d] * carry[d] + upd
        return o

    def finish(c, o):
        rows = pl.ds(c * RCH, RCH)
        gate = g_ref[rows, :].astype(F32)
        y = _rms(o) * gn_ref[...] * (gate * _sigmoid(gate))
        o_ref[rows, :] = y.astype(o_ref.dtype)

    for i in range(n):
        outs = [step(d, i) for d in range(2)]
        if order[0][i] == order[1][i]:
            finish(order[0][i], outs[0] + outs[1])
            continue
        for d in range(2):
            c = order[d][i]
            rows = pl.ds(c * RCH, RCH)
            if when[1 - d][c] < i:
                finish(c, outs[d] + part_scr[rows, :])
            else:
                part_scr[rows, :] = outs[d]


def _retention(logit, u, gn, n_lat):
    nb, t, _ = u.shape
    return pl.pallas_call(
        functools.partial(_retention_kernel, n_lat=n_lat),
        grid=(nb, C_HEADS),
        in_specs=[
            pl.BlockSpec(memory_space=pltpu.SMEM),
            pl.BlockSpec((None, t, C_DK), lambda b, h: (b, 0, U_CQ * UNIT // C_DK + h)),
            pl.BlockSpec((None, t, C_DK), lambda b, h: (b, 0, U_CK * UNIT // C_DK + h)),
            pl.BlockSpec((None, t, C_DV), lambda b, h: (b, 0, U_CV * UNIT // C_DV + h)),
            pl.BlockSpec((None, t, C_DV), lambda b, h: (b, 0, U_CG * UNIT // C_DV + h)),
            pl.BlockSpec((1, C_DV), lambda b, h: (0, h)),
        ],
        out_specs=pl.BlockSpec((None, t, C_DV), lambda b, h: (b, 0, h)),
        out_shape=jax.ShapeDtypeStruct((nb, t, C_HEADS * C_DV), BF16),
        scratch_shapes=[
            pltpu.VMEM((t, C_DV), F32),
            pltpu.VMEM((2, C_DK, C_DV), F32),
            pltpu.VMEM((2, RCH, RCH), F32),
            pltpu.VMEM((4, RCH, LANES), F32),
        ],
        compiler_params=_params("parallel", "parallel"),
        name="retention",
    )(logit, u, u, u, u, gn.reshape(1, C_HEADS * C_DV))


def _merge_kernel(ya_ref, yb_ref, yc_ref, ga_ref, gb_ref, gc_ref, x_ref, modb_ref, modc_ref, g2_ref,
                  wpa_ref, wpb_ref, wpc_ref, wo_ref, x_out_ref, h_out_ref, *, n_lat):
    rows = x_ref.shape[0]
    is_ctx = _ctx_rows(pl.program_id(1), rows, n_lat)

    def branch(y_ref, w_ref, gate_ref):
        p = jnp.dot(y_ref[...], w_ref[...], preferred_element_type=F32)
        return _sigmoid(gate_ref[...].astype(F32)) * p

    y = branch(ya_ref, wpa_ref, ga_ref) + branch(yb_ref, wpb_ref, gb_ref) + branch(yc_ref, wpc_ref, gc_ref)
    o = jnp.dot(y.astype(BF16), wo_ref[...], preferred_element_type=F32)
    x = x_ref[...] + _mod(modb_ref, modc_ref, 2, is_ctx) * o
    x_out_ref[...] = x
    h = _rms(x) * g2_ref[...]
    h = h * (1.0 + _mod(modb_ref, modc_ref, 4, is_ctx)) + _mod(modb_ref, modc_ref, 3, is_ctx)
    h_out_ref[...] = h.astype(h_out_ref.dtype)


def _layer_weight(w, layer, **kw):
    return pl.BlockSpec((None,) + w.shape[1:], lambda b, j: (layer, 0, 0), **kw)


def _row_plan(t, n_lat, latent_only):
    if latent_only:
        assert n_lat % LATENT_ROW_BLOCK == 0
        return LATENT_ROW_BLOCK, n_lat
    return ROW_BLOCK, t


def _merge(ya, yb, yc, u, x, mod, g2, wpa, wpb, wpc, wo, layer, n_lat, latent_only):
    nb, t, d = x.shape
    rb, n_rows = _row_plan(t, n_lat, latent_only)
    row = lambda b, j: (b, j, 0)
    full = lambda b, j: (0, 0)
    gate = lambda unit: pl.BlockSpec((None, rb, d), lambda b, j: (b, j, unit * UNIT // d))
    return pl.pallas_call(
        functools.partial(_merge_kernel, n_lat=n_lat),
        grid=(nb, n_rows // rb),
        in_specs=[
            pl.BlockSpec((None, rb, ya.shape[2]), row),
            pl.BlockSpec((None, rb, yb.shape[2]), row),
            pl.BlockSpec((None, rb, yc.shape[2]), row),
            gate(U_GA), gate(U_GB), gate(U_GC),
            pl.BlockSpec((None, rb, d), row),
        ] + _mod_specs(d, nb) + [
            pl.BlockSpec((1, d), full),
            _layer_weight(wpa, layer), _layer_weight(wpb, layer),
            _layer_weight(wpc, layer), _layer_weight(wo, layer),
        ],
        out_specs=[pl.BlockSpec((None, rb, d), row), pl.BlockSpec((None, rb, d), row)],
        out_shape=[jax.ShapeDtypeStruct((nb, n_rows, d), F32), jax.ShapeDtypeStruct((nb, n_rows, d), BF16)],
        compiler_params=_params("parallel", "parallel"),
        name="merge",
    )(ya, yb, yc, u, u, u, x, mod, mod, g2.reshape(1, d), wpa, wpb, wpc, wo)


def _mlp_kernel(h_ref, x_ref, modb_ref, modc_ref, nmodb_ref, nmodc_ref, gn_ref, w1_ref, w2_ref,
                *out_refs, n_lat, ff_block, final):
    rows = x_ref.shape[0]
    is_ctx = _ctx_rows(pl.program_id(1), rows, n_lat)
    h = h_ref[...]
    acc = jnp.zeros(x_ref.shape, F32)
    for f in range(w1_ref.shape[1] // ff_block):
        cols = slice(f * ff_block, (f + 1) * ff_block)
        a = jnp.maximum(jnp.dot(h, w1_ref[:, cols], preferred_element_type=F32), 0.0)
        acc = acc + jnp.dot((a * a).astype(BF16), w2_ref[cols, :], preferred_element_type=F32)
    x = x_ref[...] + _mod(modb_ref, modc_ref, 5, is_ctx) * acc
    y = _rms(x) * gn_ref[...]
    if final:
        out_refs[0][...] = y
    else:
        out_refs[0][...] = x
        hn = y * (1.0 + _mod(nmodb_ref, nmodc_ref, 1, is_ctx)) + _mod(nmodb_ref, nmodc_ref, 0, is_ctx)
        out_refs[1][...] = hn.astype(out_refs[1].dtype)


def _mlp(h, x, mod, next_mod, gn, w1, w2, layer, n_lat, final):
    nb, t, d = x.shape
    rb = LATENT_ROW_BLOCK if final else ROW_BLOCK
    n_rows = t
    row = lambda b, j: (b, j, 0)
    full = lambda b, j: (0, 0)
    block = pl.BlockSpec((None, rb, d), row)
    if final:
        out_specs = [block]
        out_shape = [jax.ShapeDtypeStruct((nb, n_rows, d), F32)]
    else:
        out_specs = [block, block]
        out_shape = [jax.ShapeDtypeStruct((nb, t, d), F32), jax.ShapeDtypeStruct((nb, t, d), BF16)]
    return pl.pallas_call(
        functools.partial(_mlp_kernel, n_lat=n_lat, ff_block=1024, final=final),
        grid=(nb, n_rows // rb),
        in_specs=[block, block] + _mod_specs(d, nb) + _mod_specs(d, nb) + [
            pl.BlockSpec((1, d), full),
            _layer_weight(w1, layer, pipeline_mode=pl.Buffered(1)),
            _layer_weight(w2, layer, pipeline_mode=pl.Buffered(1)),
        ],
        out_specs=out_specs,
        out_shape=out_shape,
        compiler_params=_params("parallel", "parallel"),
        name="mlp",
    )(h, x, mod, mod, next_mod, next_mod, gn.reshape(1, d), w1, w2)


def _rope_tables(n_lat, n_ctx):
    rows = n_lat // GRID_W
    row = jnp.arange(rows).astype(F32)
    col = jnp.arange(GRID_W).astype(F32)

    def cos_sin(dim):
        n = dim // 4
        inv = ROPE_THETA ** (-jnp.arange(n, dtype=F32) / n)
        ang_r, ang_c = row[:, None] * inv, col[:, None] * inv
        spread = lambda fn: jnp.concatenate(
            [jnp.repeat(fn(ang_r), GRID_W, axis=0), jnp.tile(fn(ang_c), (rows, 1))], axis=-1)
        return spread(jnp.cos), spread(jnp.sin)

    def with_ctx(tab, fill):
        return jnp.concatenate([tab, jnp.full((n_ctx, tab.shape[1]), fill, F32)], axis=0)

    cos_h, sin_h = cos_sin(A_HD)
    cos_r, sin_r = cos_sin(C_DK)
    return (
        with_ctx(jnp.concatenate([cos_h, cos_h], axis=-1), 1.0),
        with_ctx(jnp.concatenate([-sin_h, sin_h], axis=-1), 0.0),
        with_ctx(cos_r, 1.0),
        with_ctx(sin_r, 0.0),
    )


def kernel(x, c, ctx, c_ctx, w_ada, b_ada, norm1_g, norm2_g, w_in, qn_g, kn_g, sink, ret_logit, ret_norm_g,
           w_pa, w_pb, w_pc, w_o, w_ff1, w_ff2, final_g):
    nb, n_lat, d = x.shape
    n_ctx = ctx.shape[1]
    depth = w_ada.shape[0]
    t = n_lat + n_ctx
    assert t % ROW_BLOCK == 0 and n_lat % TQ == 0 and n_ctx % TQ == 0 and n_lat % GRID_W == 0
    assert n_lat >= TQ + 2 * WINDOW and w_in.shape[2] == N_UNITS * UNIT

    tabs = _rope_tables(n_lat, n_ctx)
    mod_rows = 16
    craw = jnp.concatenate([c, c_ctx[None], jnp.zeros((mod_rows - nb - 1, d), F32)], axis=0)
    mod = _ada(craw, w_ada, b_ada).reshape(depth, mod_rows, N_MOD, d)

    merge_w = [w.astype(BF16) for w in (w_pa, w_pb, w_pc, w_o)]
    mlp_w = [w.astype(BF16) for w in (w_ff1, w_ff2)]

    xs, h = _prenorm(x, ctx, mod[0], norm1_g[0])
    for l in range(depth):
        u = _inproj(h, w_in, l, tabs, qn_g[l], kn_g[l])
        ya, yb = _attention(sink[l], u, n_lat)
        yc = _retention(ret_logit[l], u, ret_norm_g[l], n_lat)
        final = l == depth - 1
        xs, h2 = _merge(ya, yb, yc, u, xs, mod[l], norm2_g[l], *merge_w, l, n_lat, latent_only=final)
        gn = final_g if final else norm1_g[l + 1]
        outs = _mlp(h2, xs, mod[l], mod[l if final else l + 1], gn, *mlp_w, l, n_lat, final)
        if final:
            return outs[0]
        xs, h = outs
```

```python
import functools
import math
from typing import Any, Callable, NamedTuple

import jax
import jax.numpy as jnp
from jax import lax
from jax.experimental import pallas as pl
from jax.experimental.pallas import tpu as pltpu

F32 = jnp.float32
BF16 = jnp.bfloat16

GRID_W = 64
WINDOW = 128
ROPE_THETA = 10000.0
EPS = 1e-6
NEG = -1e30
A_HEADS, A_KV, A_HD = 8, 2, 128
B_HEADS, B_KV, B_HD = 8, 2, 128
C_HEADS, C_DK, C_DV = 4, 256, 512
N_MOD = 6
LOG2E = math.log2(math.e)

LANES = 128
MXU_DIM = 256
V7X_VMEM_BYTES = 64 * 1024 * 1024
VMEM_LIMIT = V7X_VMEM_BYTES - 8 * 1024 * 1024

ROW_BLOCK = 768
LATENT_ROW_BLOCK = 512
TQ = MXU_DIM
KCH = MXU_DIM
RCH = MXU_DIM
UNIT = LANES

U_AQ, U_BQ = 0, 8
U_AK, U_BK, U_AV, U_BV = 16, 18, 20, 22
U_CQ, U_CK, U_CV, U_CG = 24, 32, 40, 56
U_GA, U_GB, U_GC = 72, 80, 88
N_UNITS = 96
_UNIT_PERM = (list(range(0, 8)) + list(range(12, 20)) + [8, 9, 20, 21, 10, 11, 22, 23]
              + list(range(24, 96)))
IN_BLOCK_UNITS = 8
IN_PAIR = MXU_DIM // UNIT


def _params(*sem):
    return pltpu.CompilerParams(dimension_semantics=sem, vmem_limit_bytes=VMEM_LIMIT)


def _rms(x):
    return x * lax.rsqrt(jnp.mean(x * x, axis=-1, keepdims=True) + EPS)


def _sigmoid(x):
    return 1.0 / (1.0 + jnp.exp2(x * -LOG2E))


def _ctx_rows(j, rows, n_lat):
    r = j * rows + lax.broadcasted_iota(jnp.int32, (rows, 1), 0)
    return r >= n_lat


def _mod(modb_ref, modc_ref, idx, is_ctx):
    return jnp.where(is_ctx, modc_ref[idx:idx + 1, :], modb_ref[idx:idx + 1, :])


def _ada_kernel(c_ref, w_ref, b_ref, o_ref):
    c = c_ref[...]
    s = (c * _sigmoid(c)).astype(BF16)
    o_ref[...] = jnp.dot(s, w_ref[...].astype(BF16), preferred_element_type=F32) + b_ref[...]


def _ada(craw, w_ada, b_ada):
    depth, d, width = w_ada.shape
    rows = craw.shape[0]
    tn = 1536
    return pl.pallas_call(
        _ada_kernel,
        grid=(depth, width // tn),
        in_specs=[
            pl.BlockSpec((rows, d), lambda l, j: (0, 0)),
            pl.BlockSpec((None, d, tn), lambda l, j: (l, 0, j)),
            pl.BlockSpec((None, 1, tn), lambda l, j: (l, 0, j)),
        ],
        out_specs=pl.BlockSpec((None, rows, tn), lambda l, j: (l, 0, j)),
        out_shape=jax.ShapeDtypeStruct((depth, rows, width), F32),
        compiler_params=_params("parallel", "parallel"),
        name="ada",
    )(craw, w_ada, b_ada.reshape(depth, 1, width))


def _prenorm_kernel(x_ref, ctx_ref, modb_ref, modc_ref, g_ref, xs_ref, h_ref, *, n_lat):
    rows = x_ref.shape[0]
    j = pl.program_id(1)
    is_ctx = _ctx_rows(j, rows, n_lat)
    x = jnp.where(j * rows < n_lat, x_ref[...], ctx_ref[...])
    xs_ref[...] = x
    y = _rms(x) * g_ref[...]
    h = y * (1.0 + _mod(modb_ref, modc_ref, 1, is_ctx)) + _mod(modb_ref, modc_ref, 0, is_ctx)
    h_ref[...] = h.astype(h_ref.dtype)


def _mod_specs(d, n_batch):
    return [
        pl.BlockSpec((None, N_MOD, d), lambda b, j: (b, 0, 0)),
        pl.BlockSpec((None, N_MOD, d), lambda b, j: (n_batch, 0, 0)),
    ]


def _prenorm(x, ctx, mod, g):
    nb, n_lat, d = x.shape
    n_ctx = ctx.shape[1]
    rows = math.gcd(n_lat, n_ctx)
    n_lat_blocks = n_lat // rows
    row = lambda b, j: (b, j, 0)
    block = pl.BlockSpec((None, rows, d), row)
    return pl.pallas_call(
        functools.partial(_prenorm_kernel, n_lat=n_lat),
        grid=(nb, (n_lat + n_ctx) // rows),
        in_specs=[
            pl.BlockSpec((None, rows, d), lambda b, j: (b, jnp.minimum(j, n_lat_blocks - 1), 0)),
            pl.BlockSpec((None, rows, d), lambda b, j: (b, jnp.maximum(j - n_lat_blocks, 0), 0)),
        ] + _mod_specs(d, nb) + [pl.BlockSpec((1, d), lambda b, j: (0, 0))],
        out_specs=[block, block],
        out_shape=[jax.ShapeDtypeStruct((nb, n_lat + n_ctx, d), F32),
                   jax.ShapeDtypeStruct((nb, n_lat + n_ctx, d), BF16)],
        compiler_params=_params("parallel", "parallel"),
        name="prenorm",
    )(x, ctx, mod, mod, g.reshape(1, d))


def _rope128(a, c, s):
    return a * c + pltpu.roll(a, LANES // 2, axis=1) * s


def _rms_heads(a):
    r = lax.broadcasted_iota(jnp.int32, (MXU_DIM, MXU_DIM), 0) // LANES
    c = lax.broadcasted_iota(jnp.int32, (MXU_DIM, MXU_DIM), 1) // LANES
    ones = (r == c).astype(BF16)
    tot = jnp.dot((a * a).astype(BF16), ones, preferred_element_type=F32)
    return a * lax.rsqrt(tot * (1.0 / LANES) + EPS)


def _inproj_kernel(src_ref, h_ref, *refs):
    del src_ref
    n_pairs = IN_BLOCK_UNITS // IN_PAIR
    w_refs = refs[:n_pairs]
    c128_ref, s128_ref, c256_ref, s256_ref, qn_ref, kn_ref, o_ref, w_ref = refs[n_pairs:]
    j = pl.program_id(0)
    t = h_ref.shape[0]

    @pl.when(pl.program_id(1) == 0)
    def _():
        for p, wp_ref in enumerate(w_refs):
            w_ref[:, p * MXU_DIM:(p + 1) * MXU_DIM] = wp_ref[...].astype(w_ref.dtype)

    def for_pairs(fn):
        steps = [(pl.ds(ci * ROW_BLOCK, ROW_BLOCK), u)
                 for ci in range(t // ROW_BLOCK) for u in range(0, IN_BLOCK_UNITS, IN_PAIR)]
        pending = None
        for rows, u in steps + [(None, None)]:
            if rows is not None:
                cols = slice(u * UNIT, (u + IN_PAIR) * UNIT)
                a = jnp.dot(h_ref[rows, :], w_ref[:, cols], preferred_element_type=F32)
            if pending is not None:
                p_rows, p_u, p_a = pending
                o_ref[p_rows, p_u * UNIT:(p_u + IN_PAIR) * UNIT] = fn(p_u, p_a, p_rows).astype(o_ref.dtype)
            pending = (rows, u, a) if rows is not None else None

    def rope_heads(a, rows, gain=None, scale=None):
        out = []
        for a1 in (a[:, :UNIT], a[:, UNIT:]):
            if gain is not None:
                a1 = a1 * gain
            a1 = _rope128(a1, c128_ref[rows, :], s128_ref[rows, :])
            out.append(a1 if scale is None else a1 * scale)
        return jnp.concatenate(out, axis=1)

    @pl.when(j == U_AQ // IN_BLOCK_UNITS)
    def _():
        scale = A_HD ** -0.5 * LOG2E
        for_pairs(lambda u, a, rows: rope_heads(_rms_heads(a), rows, qn_ref[...], scale))

    @pl.when(j == U_BQ // IN_BLOCK_UNITS)
    def _():
        scale = B_HD ** -0.5 * LOG2E
        for_pairs(lambda u, a, rows: rope_heads(a, rows, scale=scale))

    @pl.when(j == U_AK // IN_BLOCK_UNITS)
    def _():
        def fn(u, a, rows):
            if u < U_BK - U_AK:
                return rope_heads(_rms_heads(a), rows, kn_ref[...])
            if u < U_AV - U_AK:
                return rope_heads(a, rows)
            return a
        for_pairs(fn)

    def rope_r(scale):
        def fn(u, a, rows):
            a0, a1 = a[:, :UNIT], a[:, UNIT:]
            c, s = c256_ref[rows, :], s256_ref[rows, :]
            return jnp.concatenate([(a0 * c - a1 * s) * scale, (a0 * s + a1 * c) * scale], axis=1)
        return fn

    @pl.when(j == U_CQ // IN_BLOCK_UNITS)
    def _():
        for_pairs(rope_r(1.0))

    @pl.when(j == U_CK // IN_BLOCK_UNITS)
    def _():
        for_pairs(rope_r(C_DK ** -0.5))

    @pl.when(j >= U_CV // IN_BLOCK_UNITS)
    def _():
        for_pairs(lambda u, a, rows: a)


def _inproj(h, w_in, layer, tabs, qn, kn):
    nb, t, d = h.shape
    width = w_in.shape[2]
    tn = IN_BLOCK_UNITS * UNIT
    n_pairs = IN_BLOCK_UNITS // IN_PAIR
    assert all(a % IN_PAIR == 0 and b == a + 1 for a, b in zip(_UNIT_PERM[0::2], _UNIT_PERM[1::2]))
    src = jnp.asarray([a // IN_PAIR for a in _UNIT_PERM[0::IN_PAIR]], jnp.int32)
    tab = pl.BlockSpec((t, LANES), lambda j, b, src: (0, 0))
    gain = pl.BlockSpec((1, LANES), lambda j, b, src: (0, 0))

    def pair(p):
        return pl.BlockSpec((None, d, MXU_DIM), lambda j, b, src: (layer, 0, src[j * n_pairs + p]))

    return pl.pallas_call(
        _inproj_kernel,
        grid_spec=pltpu.PrefetchScalarGridSpec(
            num_scalar_prefetch=1,
            grid=(width // tn, nb),
            in_specs=[pl.BlockSpec((None, t, d), lambda j, b, src: (b, 0, 0))]
            + [pair(p) for p in range(n_pairs)] + [tab, tab, tab, tab, gain, gain],
            out_specs=pl.BlockSpec((None, t, tn), lambda j, b, src: (b, 0, j)),
            scratch_shapes=[pltpu.VMEM((d, tn), BF16)],
        ),
        out_shape=jax.ShapeDtypeStruct((nb, t, width), BF16),
        compiler_params=_params("arbitrary", "arbitrary"),
        name="inproj",
    )(src, h, *([w_in] * n_pairs), *tabs, qn.reshape(1, LANES), kn.reshape(1, LANES))


def _scores_t(k, q):
    return lax.dot_general(k, q, (((1,), (1,)), ((), ())), preferred_element_type=F32)


def _stack_heads(q_ref, group, hd):
    return jnp.concatenate([q_ref[:, g * hd:(g + 1) * hd] for g in range(group)], axis=0)


def _store_heads(o_ref, o, group, hd):
    for g in range(group):
        o_ref[:, g * hd:(g + 1) * hd] = o[g * TQ:(g + 1) * TQ, :].astype(o_ref.dtype)


class _Stream(NamedTuple):
    q_ref: Any
    o_ref: Any
    s_scr: Any
    m_scr: Any
    lat_chunks: Any
    ctx_chunks: Any
    keys: Callable
    values_t: Callable
    bias: Callable
    sink: Any


def _softmax_pipeline(i, n_q_lat, streams, group, hd):
    def rows(c):
        return pl.ds(c * KCH, KCH)

    def score_chunk(st, q, c, m):
        s = _scores_t(st.keys(c, i), q)
        b = st.bias(c, i)
        if b is not None:
            s = jnp.concatenate([s[:, g * TQ:(g + 1) * TQ] + b for g in range(group)], axis=1)
        st.s_scr[rows(c), :] = s
        mc = jnp.max(s, axis=0, keepdims=True)
        return mc if m is None else jnp.maximum(m, mc)

    def value_chunk(st, c, m, den, o):
        p = jnp.exp2(st.s_scr[rows(c), :] - m)
        dc = jnp.sum(p, axis=0, keepdims=True)
        oc = jnp.dot(st.values_t(c, i - 1), p.astype(BF16), preferred_element_type=F32)
        return (dc if den is None else den + dc), (oc if o is None else o + oc)

    def run(score_kind, value_kind):
        work, state = [], []
        for si, st in enumerate(streams):
            sc = set(getattr(st, score_kind)) if score_kind else set()
            vc = set(getattr(st, value_kind)) if value_kind else set()
            m_old = st.m_scr[0:1, :] if vc else None
            den = jnp.exp2(st.sink - m_old) if vc and st.sink is not None else None
            state.append(dict(sc=sc, vc=vc, m_old=m_old, den=den, o=None, m_new=st.sink,
                              q=_stack_heads(st.q_ref, group, hd) if sc else None))
            chunks = sorted(sc | vc)
            work += [((k + 0.5) / len(chunks), si, c) for k, c in enumerate(chunks)]
        for _, si, c in sorted(work):
            st, z = streams[si], state[si]
            if c in z["vc"]:
                z["den"], z["o"] = value_chunk(st, c, z["m_old"], z["den"], z["o"])
            if c in z["sc"]:
                z["m_new"] = score_chunk(st, z["q"], c, z["m_new"])
        for st, z in zip(streams, state):
            if z["vc"]:
                _store_heads(st.o_ref, (z["o"] * (1.0 / z["den"])).T, group, hd)
            if z["sc"]:
                st.m_scr[0:1, :] = z["m_new"]

    @pl.when(i == 0)
    def _():
        run("lat_chunks", None)

    @pl.when((i > 0) & (i < n_q_lat))
    def _():
        run("lat_chunks", "lat_chunks")

    @pl.when(i == n_q_lat)
    def _():
        run("ctx_chunks", "lat_chunks")

    @pl.when(i > n_q_lat)
    def _():
        run(None, "ctx_chunks")


def _attn_kernel(sink_ref, qa_ref, ka_ref, va_ref, qb_ref, kb_ref, vb_ref, oa_ref, ob_ref,
                 vt_scr, sa_scr, ma_scr, band_scr, sb_scr, mb_scr, *, n_lat, group, hd):
    kv = pl.program_id(1)
    i = pl.program_id(2)
    t = ka_ref.shape[0]
    span = TQ + 2 * WINDOW
    sink = jnp.concatenate(
        [jnp.full((1, TQ), sink_ref[kv * group + g], F32) for g in range(group)], axis=1) * LOG2E

    @pl.when(i == 0)
    def _():
        vt_scr[...] = va_ref[...].T
        r = lax.broadcasted_iota(jnp.int32, (span, TQ), 0)
        c = lax.broadcasted_iota(jnp.int32, (span, TQ), 1)
        for lead in range(band_scr.shape[0]):
            band_scr[lead] = jnp.where(jnp.abs(c + lead * WINDOW - r) <= WINDOW, 0.0, NEG)

    def start(blk):
        return jnp.clip(blk * TQ - WINDOW, 0, n_lat - span)

    def key_rows(c, blk):
        if c == 0:
            return pl.ds(n_lat, KCH)
        return pl.ds(pl.multiple_of(start(blk) + (c - 1) * KCH, WINDOW), KCH)

    def band(c, blk):
        if c == 0:
            return None
        return band_scr[(blk * TQ - start(blk)) // WINDOW, pl.ds((c - 1) * KCH, KCH), :]

    mixer_a = _Stream(
        qa_ref, oa_ref, sa_scr, ma_scr,
        lat_chunks=range(t // KCH), ctx_chunks=range(n_lat // KCH, t // KCH),
        keys=lambda c, blk: ka_ref[pl.ds(c * KCH, KCH), :],
        values_t=lambda c, blk: vt_scr[:, pl.ds(c * KCH, KCH)],
        bias=lambda c, blk: None, sink=None)
    mixer_b = _Stream(
        qb_ref, ob_ref, sb_scr, mb_scr,
        lat_chunks=range(1 + span // KCH), ctx_chunks=range(1),
        keys=lambda c, blk: kb_ref[key_rows(c, blk), :],
        values_t=lambda c, blk: vb_ref[key_rows(c, blk), :].T,
        bias=band, sink=sink)
    _softmax_pipeline(i, n_lat // TQ, (mixer_a, mixer_b), group, hd)


def _attention(sink, u, n_lat):
    nb, t, _ = u.shape
    assert (A_HEADS, A_KV, A_HD) == (B_HEADS, B_KV, B_HD), "the mixers share one grid"
    group = A_HEADS // A_KV
    hd = A_HD
    qw = group * hd
    n_q = t // TQ
    span = TQ + 2 * WINDOW
    assert n_q == n_lat // TQ + 1 and t - n_lat == KCH and span % KCH == 0

    def q_spec(unit):
        return pl.BlockSpec((None, TQ, qw),
                            lambda b, kv, i: (b, jnp.minimum(i, n_q - 1), unit * UNIT // qw + kv))

    def kv_spec(unit):
        return pl.BlockSpec((None, t, hd), lambda b, kv, i: (b, 0, unit + kv))

    out_spec = pl.BlockSpec((None, TQ, qw), lambda b, kv, i: (b, jnp.maximum(i - 1, 0), kv))
    out = jax.ShapeDtypeStruct((nb, t, A_HEADS * hd), BF16)
    return pl.pallas_call(
        functools.partial(_attn_kernel, n_lat=n_lat, group=group, hd=hd),
        grid=(nb, A_KV, n_q + 1),
        in_specs=[pl.BlockSpec(memory_space=pltpu.SMEM),
                  q_spec(U_AQ), kv_spec(U_AK), kv_spec(U_AV),
                  q_spec(U_BQ), kv_spec(U_BK), kv_spec(U_BV)],
        out_specs=[out_spec, out_spec],
        out_shape=[out, out],
        scratch_shapes=[
            pltpu.VMEM((hd, t), BF16),
            pltpu.VMEM((t, group * TQ), F32),
            pltpu.VMEM((8, group * TQ), F32),
            pltpu.VMEM((3, span, TQ), F32),
            pltpu.VMEM((KCH + span, group * TQ), F32),
            pltpu.VMEM((8, group * TQ), F32),
        ],
        compiler_params=_params("parallel", "parallel", "arbitrary"),
        name="attention",
    )(sink, u, u, u, u, u, u)


def _scale_rows(x, v):
    return jnp.concatenate(
        [x[:, u * LANES:(u + 1) * LANES] * v for u in range(x.shape[1] // LANES)], axis=1)


def _retention_kernel(logit_ref, q_ref, k_ref, v_ref, g_ref, gn_ref, o_ref,
                      part_scr, state_scr, decay_scr, vec_scr, *, n_lat):
    h = pl.program_id(1)
    t = q_ref.shape[0]
    n_l, n_c = n_lat // RCH, (t - n_lat) // RCH
    n = n_l + n_c
    order = ([n_l + c for c in range(n_c)] + list(range(n_l)),
             [n_l + c for c in reversed(range(n_c))] + list(reversed(range(n_l))))
    when = tuple({c: i for i, c in enumerate(o)} for o in order)

    def log_gamma(direction, width):
        x = jnp.full((1, width), logit_ref[direction, h], F32)
        return jnp.minimum(x, 0.0) - jnp.log1p(jnp.exp(-jnp.abs(x)))

    qi = lax.broadcasted_iota(jnp.int32, (RCH, RCH), 0).astype(F32)
    si = lax.broadcasted_iota(jnp.int32, (RCH, RCH), 1).astype(F32)
    ri = lax.broadcasted_iota(jnp.int32, (RCH, LANES), 0).astype(F32)
    carry = []
    for d in range(2):
        lag = qi - si if d == 0 else si - qi
        decay_scr[d] = jnp.where(lag >= 0, jnp.exp(jnp.maximum(lag, 0.0) * log_gamma(d, RCH)), 0.0)
        lg = log_gamma(d, LANES)
        vec_scr[2 * d] = jnp.exp((ri + 1.0 if d == 0 else RCH - ri) * lg)
        vec_scr[2 * d + 1] = jnp.exp((RCH - 1.0 - ri if d == 0 else ri) * lg)
        carry.append(jnp.exp(RCH * log_gamma(d, C_DV)))

    def step(d, i):
        rows = pl.ds(order[d][i] * RCH, RCH)
        qb, kb, vb = q_ref[rows, :], k_ref[rows, :], v_ref[rows, :]
        att = _scores_t(qb, kb) * decay_scr[d]
        o = jnp.dot(att.astype(BF16), vb, preferred_element_type=F32)
        if i > 0:
            qd = _scale_rows(qb.astype(F32), vec_scr[2 * d]).astype(BF16)
            o = o + jnp.dot(qd, state_scr[d].astype(BF16), preferred_element_type=F32)
        if i < n - 1:
            kd = _scale_rows(kb.astype(F32), vec_scr[2 * d + 1]).astype(BF16)
            upd = lax.dot_general(kd, vb, (((0,), (0,)), ((), ())), preferred_element_type=F32)
            state_scr[d] = upd if i == 0 else state_scr[d] * carry[d] + upd
        return o

    def finish(c, o):
        rows = pl.ds(c * RCH, RCH)
        gate = g_ref[rows, :].astype(F32)
        y = _rms(o) * gn_ref[...] * (gate * _sigmoid(gate))
        o_ref[rows, :] = y.astype(o_ref.dtype)

    for i in range(n):
        outs = [step(d, i) for d in range(2)]
        if order[0][i] == order[1][i]:
            finish(order[0][i], outs[0] + outs[1])
            continue
        for d in range(2):
            c = order[d][i]
            rows = pl.ds(c * RCH, RCH)
            if when[1 - d][c] < i:
                finish(c, outs[d] + part_scr[rows, :])
            else:
                part_scr[rows, :] = outs[d]


def _retention(logit, u, gn, n_lat):
    nb, t, _ = u.shape
    return pl.pallas_call(
        functools.partial(_retention_kernel, n_lat=n_lat),
        grid=(nb, C_HEADS),
        in_specs=[
            pl.BlockSpec(memory_space=pltpu.SMEM),
            pl.BlockSpec((None, t, C_DK), lambda b, h: (b, 0, U_CQ * UNIT // C_DK + h)),
            pl.BlockSpec((None, t, C_DK), lambda b, h: (b, 0, U_CK * UNIT // C_DK + h)),
            pl.BlockSpec((None, t, C_DV), lambda b, h: (b, 0, U_CV * UNIT // C_DV + h)),
            pl.BlockSpec((None, t, C_DV), lambda b, h: (b, 0, U_CG * UNIT // C_DV + h)),
            pl.BlockSpec((1, C_DV), lambda b, h: (0, h)),
        ],
        out_specs=pl.BlockSpec((None, t, C_DV), lambda b, h: (b, 0, h)),
        out_shape=jax.ShapeDtypeStruct((nb, t, C_HEADS * C_DV), BF16),
        scratch_shapes=[
            pltpu.VMEM((t, C_DV), F32),
            pltpu.VMEM((2, C_DK, C_DV), F32),
            pltpu.VMEM((2, RCH, RCH), F32),
            pltpu.VMEM((4, RCH, LANES), F32),
        ],
        compiler_params=_params("parallel", "parallel"),
        name="retention",
    )(logit, u, u, u, u, gn.reshape(1, C_HEADS * C_DV))


def _merge_kernel(ya_ref, yb_ref, yc_ref, ga_ref, gb_ref, gc_ref, x_ref, modb_ref, modc_ref, g2_ref,
                  wpa_ref, wpb_ref, wpc_ref, wo_ref, x_out_ref, h_out_ref, *, n_lat):
    rows = x_ref.shape[0]
    is_ctx = _ctx_rows(pl.program_id(1), rows, n_lat)

    def branch(y_ref, w_ref, gate_ref):
        p = jnp.dot(y_ref[...], w_ref[...], preferred_element_type=F32)
        return _sigmoid(gate_ref[...].astype(F32)) * p

    y = branch(ya_ref, wpa_ref, ga_ref) + branch(yb_ref, wpb_ref, gb_ref) + branch(yc_ref, wpc_ref, gc_ref)
    o = jnp.dot(y.astype(BF16), wo_ref[...], preferred_element_type=F32)
    x = x_ref[...] + _mod(modb_ref, modc_ref, 2, is_ctx) * o
    x_out_ref[...] = x
    h = _rms(x) * g2_ref[...]
    h = h * (1.0 + _mod(modb_ref, modc_ref, 4, is_ctx)) + _mod(modb_ref, modc_ref, 3, is_ctx)
    h_out_ref[...] = h.astype(h_out_ref.dtype)


def _layer_weight(w, layer, **kw):
    return pl.BlockSpec((None,) + w.shape[1:], lambda b, j: (layer, 0, 0), **kw)


def _row_plan(t, n_lat, latent_only):
    if latent_only:
        assert n_lat % LATENT_ROW_BLOCK == 0
        return LATENT_ROW_BLOCK, n_lat
    return ROW_BLOCK, t


def _merge(ya, yb, yc, u, x, mod, g2, wpa, wpb, wpc, wo, layer, n_lat, latent_only):
    nb, t, d = x.shape
    rb, n_rows = _row_plan(t, n_lat, latent_only)
    row = lambda b, j: (b, j, 0)
    full = lambda b, j: (0, 0)
    gate = lambda unit: pl.BlockSpec((None, rb, d), lambda b, j: (b, j, unit * UNIT // d))
    return pl.pallas_call(
        functools.partial(_merge_kernel, n_lat=n_lat),
        grid=(nb, n_rows // rb),
        in_specs=[
            pl.BlockSpec((None, rb, ya.shape[2]), row),
            pl.BlockSpec((None, rb, yb.shape[2]), row),
            pl.BlockSpec((None, rb, yc.shape[2]), row),
            gate(U_GA), gate(U_GB), gate(U_GC),
            pl.BlockSpec((None, rb, d), row),
        ] + _mod_specs(d, nb) + [
            pl.BlockSpec((1, d), full),
            _layer_weight(wpa, layer), _layer_weight(wpb, layer),
            _layer_weight(wpc, layer), _layer_weight(wo, layer),
        ],
        out_specs=[pl.BlockSpec((None, rb, d), row), pl.BlockSpec((None, rb, d), row)],
        out_shape=[jax.ShapeDtypeStruct((nb, n_rows, d), F32), jax.ShapeDtypeStruct((nb, n_rows, d), BF16)],
        compiler_params=_params("parallel", "parallel"),
        name="merge",
    )(ya, yb, yc, u, u, u, x, mod, mod, g2.reshape(1, d), wpa, wpb, wpc, wo)


def _mlp_kernel(h_ref, x_ref, modb_ref, modc_ref, nmodb_ref, nmodc_ref, gn_ref, w1_ref, w2_ref,
                *out_refs, n_lat, ff_block, final):
    rows = x_ref.shape[0]
    is_ctx = _ctx_rows(pl.program_id(1), rows, n_lat)
    h = h_ref[...]
    acc = jnp.zeros(x_ref.shape, F32)
    for f in range(w1_ref.shape[1] // ff_block):
        cols = slice(f * ff_block, (f + 1) * ff_block)
        a = jnp.maximum(jnp.dot(h, w1_ref[:, cols], preferred_element_type=F32), 0.0)
        acc = acc + jnp.dot((a * a).astype(BF16), w2_ref[cols, :], preferred_element_type=F32)
    x = x_ref[...] + _mod(modb_ref, modc_ref, 5, is_ctx) * acc
    y = _rms(x) * gn_ref[...]
    if final:
        out_refs[0][...] = y
    else:
        out_refs[0][...] = x
        hn = y * (1.0 + _mod(nmodb_ref, nmodc_ref, 1, is_ctx)) + _mod(nmodb_ref, nmodc_ref, 0, is_ctx)
        out_refs[1][...] = hn.astype(out_refs[1].dtype)


def _mlp(h, x, mod, next_mod, gn, w1, w2, layer, n_lat, final):
    nb, t, d = x.shape
    rb = LATENT_ROW_BLOCK if final else ROW_BLOCK
    n_rows = t
    row = lambda b, j: (b, j, 0)
    full = lambda b, j: (0, 0)
    block = pl.BlockSpec((None, rb, d), row)
    if final:
        out_specs = [block]
        out_shape = [jax.ShapeDtypeStruct((nb, n_rows, d), F32)]
    else:
        out_specs = [block, block]
        out_shape = [jax.ShapeDtypeStruct((nb, t, d), F32), jax.ShapeDtypeStruct((nb, t, d), BF16)]
    return pl.pallas_call(
        functools.partial(_mlp_kernel, n_lat=n_lat, ff_block=1024, final=final),
        grid=(nb, n_rows // rb),
        in_specs=[block, block] + _mod_specs(d, nb) + _mod_specs(d, nb) + [
            pl.BlockSpec((1, d), full),
            _layer_weight(w1, layer, pipeline_mode=pl.Buffered(1)),
            _layer_weight(w2, layer, pipeline_mode=pl.Buffered(1)),
        ],
        out_specs=out_specs,
        out_shape=out_shape,
        compiler_params=_params("parallel", "parallel"),
        name="mlp",
    )(h, x, mod, mod, next_mod, next_mod, gn.reshape(1, d), w1, w2)


def _rope_tables(n_lat, n_ctx):
    rows = n_lat // GRID_W
    row = jnp.arange(rows).astype(F32)
    col = jnp.arange(GRID_W).astype(F32)

    def cos_sin(dim):
        n = dim // 4
        inv = ROPE_THETA ** (-jnp.arange(n, dtype=F32) / n)
        ang_r, ang_c = row[:, None] * inv, col[:, None] * inv
        spread = lambda fn: jnp.concatenate(
            [jnp.repeat(fn(ang_r), GRID_W, axis=0), jnp.tile(fn(ang_c), (rows, 1))], axis=-1)
        return spread(jnp.cos), spread(jnp.sin)

    def with_ctx(tab, fill):
        return jnp.concatenate([tab, jnp.full((n_ctx, tab.shape[1]), fill, F32)], axis=0)

    cos_h, sin_h = cos_sin(A_HD)
    cos_r, sin_r = cos_sin(C_DK)
    return (
        with_ctx(jnp.concatenate([cos_h, cos_h], axis=-1), 1.0),
        with_ctx(jnp.concatenate([-sin_h, sin_h], axis=-1), 0.0),
        with_ctx(cos_r, 1.0),
        with_ctx(sin_r, 0.0),
    )


def kernel(x, c, ctx, c_ctx, w_ada, b_ada, norm1_g, norm2_g, w_in, qn_g, kn_g, sink, ret_logit, ret_norm_g,
           w_pa, w_pb, w_pc, w_o, w_ff1, w_ff2, final_g):
    nb, n_lat, d = x.shape
    n_ctx = ctx.shape[1]
    depth = w_ada.shape[0]
    t = n_lat + n_ctx
    assert t % ROW_BLOCK == 0 and n_lat % TQ == 0 and n_ctx % TQ == 0 and n_lat % GRID_W == 0
    assert n_lat >= TQ + 2 * WINDOW and w_in.shape[2] == N_UNITS * UNIT

    tabs = _rope_tables(n_lat, n_ctx)
    mod_rows = 16
    craw = jnp.concatenate([c, c_ctx[None], jnp.zeros((mod_rows - nb - 1, d), F32)], axis=0)
    mod = _ada(craw, w_ada, b_ada).reshape(depth, mod_rows, N_MOD, d)

    merge_w = [w.astype(BF16) for w in (w_pa, w_pb, w_pc, w_o)]
    mlp_w = [w.astype(BF16) for w in (w_ff1, w_ff2)]

    xs, h = _prenorm(x, ctx, mod[0], norm1_g[0])
    for l in range(depth):
        u = _inproj(h, w_in, l, tabs, qn_g[l], kn_g[l])
        ya, yb = _attention(sink[l], u, n_lat)
        yc = _retention(ret_logit[l], u, ret_norm_g[l], n_lat)
        final = l == depth - 1
        xs, h2 = _merge(ya, yb, yc, u, xs, mod[l], norm2_g[l], *merge_w, l, n_lat, latent_only=final)
        gn = final_g if final else norm1_g[l + 1]
        outs = _mlp(h2, xs, mod[l], mod[l if final else l + 1], gn, *mlp_w, l, n_lat, final)
        if final:
            return outs[0]
        xs, h = outs
```
